```python
import math
import jax, jax.numpy as jnp
from jax import lax
import numpy as np

D_MODEL = 2048
BATCH = 4
SEQ = 4096
DEPTH = 2

CHUNK = 64
N_MIXERS = 2
N_A = (DEPTH + 1) // 2
N_B = DEPTH // 2
S5_GROUP = 16
S5_GROUPS = D_MODEL // S5_GROUP
S5_STATE = 64
S5_DT_MIN = 0.001
S5_DT_MAX = 0.1
SB_HEAD_DIM = 128
SB_HEADS = D_MODEL // SB_HEAD_DIM
Q_BLOCK = 128
D_FF = 5632
FFN_RES = 0.5
EPS = 1e-6

kernel_name = "hybrid_s5_stickbreaking_macaron"


def rmsnorm(x, g):
    xf = x.astype(jnp.float32)
    y = xf * lax.rsqrt(jnp.mean(xf * xf, axis=-1, keepdims=True) + EPS)
    return (y * g.astype(jnp.float32)).astype(x.dtype)


def swiglu(h, w_gate, w_up, w_down):
    return (jax.nn.silu(h @ w_gate) * (h @ w_up)) @ w_down


def _complex_scan_combine(left, right):
    a1r, a1i, b1r, b1i = left
    a2r, a2i, b2r, b2i = right
    return (a2r * a1r - a2i * a1i,
            a2r * a1i + a2i * a1r,
            a2r * b1r - a2i * b1i + b2r,
            a2r * b1i + a2i * b1r + b2i)


def s5_mixer(h, w_in, lam_re, lam_im, log_dt, b_re, b_im, c_re, c_im, d_skip, w_glu, b_glu, w_out):
    f32 = jnp.float32
    bsz, seq, _ = h.shape
    n_chunks = seq // CHUNK
    u = (h @ w_in).astype(f32)
    dt = jnp.exp(log_dt.astype(f32))[:, None]
    lr = jnp.minimum(lam_re.astype(f32), -1e-4)
    li = lam_im.astype(f32)
    mag = jnp.exp(lr * dt)
    ab_re = mag * jnp.cos(li * dt)
    ab_im = mag * jnp.sin(li * dt)
    den = lr * lr + li * li
    n_re = ab_re - 1.0
    f_re = (n_re * lr + ab_im * li) / den
    f_im = (ab_im * lr - n_re * li) / den
    br = b_re.astype(f32)
    bi = b_im.astype(f32)
    bb_re = f_re[..., None] * br - f_im[..., None] * bi
    bb_im = f_re[..., None] * bi + f_im[..., None] * br
    cr = c_re.astype(f32)
    ci = c_im.astype(f32)

    uc = u.reshape(bsz, n_chunks, CHUNK, S5_GROUPS, S5_GROUP).transpose(1, 0, 2, 3, 4)
    a_re = jnp.broadcast_to(ab_re, (bsz, CHUNK, S5_GROUPS, S5_STATE))
    a_im = jnp.broadcast_to(ab_im, (bsz, CHUNK, S5_GROUPS, S5_STATE))

    def chunk_step(carry, u_chunk):
        s_re0, s_im0 = carry
        bu_re = jnp.einsum('bcgh,gph->bcgp', u_chunk, bb_re)
        bu_im = jnp.einsum('bcgh,gph->bcgp', u_chunk, bb_im)
        bu_re = bu_re.at[:, 0].add(ab_re * s_re0 - ab_im * s_im0)
        bu_im = bu_im.at[:, 0].add(ab_re * s_im0 + ab_im * s_re0)
        _, _, s_re, s_im = lax.associative_scan(_complex_scan_combine, (a_re, a_im, bu_re, bu_im), axis=1)
        y = jnp.einsum('bcgp,ghp->bcgh', s_re, cr) - jnp.einsum('bcgp,ghp->bcgh', s_im, ci)
        return (s_re[:, -1], s_im[:, -1]), y

    s0 = jnp.zeros((bsz, S5_GROUPS, S5_STATE), f32)
    _, yc = lax.scan(chunk_step, (s0, s0), uc)
    y = yc.transpose(1, 0, 2, 3, 4).reshape(bsz, seq, S5_GROUPS * S5_GROUP)
    y = y + d_skip.astype(f32) * u
    y = jax.nn.gelu(y).astype(h.dtype)
    y = y * jax.nn.sigmoid(y @ w_glu + b_glu)
    return y @ w_out


def stick_breaking_mixer(h, w_qkv, g_q, g_k, w_o):
    f32 = jnp.float32
    bsz, seq, _ = h.shape
    qkv = (h @ w_qkv).reshape(bsz, seq, 3, SB_HEADS, SB_HEAD_DIM)
    q = rmsnorm(qkv[:, :, 0], g_q).transpose(0, 2, 1, 3)
    k = rmsnorm(qkv[:, :, 1], g_k).transpose(0, 2, 1, 3)
    v = qkv[:, :, 2].transpose(0, 2, 1, 3)
    scale = 1.0 / math.sqrt(SB_HEAD_DIM)
    outs = []
    for blk in range(seq // Q_BLOCK):
        q0 = blk * Q_BLOCK
        kv_len = q0 + Q_BLOCK
        qb = q[:, :, q0:kv_len].astype(f32)
        kb = k[:, :, :kv_len].astype(f32)
        vb = v[:, :, :kv_len]
        z = jnp.einsum('bhqd,bhkd->bhqk', qb, kb) * scale
        t_idx = q0 + jnp.arange(Q_BLOCK)[:, None]
        s_idx = jnp.arange(kv_len)[None, :]
        past = s_idx < t_idx
        log_beta = jax.nn.log_sigmoid(z)
        log_keep = jnp.where(past, jax.nn.log_sigmoid(-z), 0.0)
        log_stick = lax.cumsum(log_keep, axis=3, reverse=True) - log_keep
        w = jnp.where(past, jnp.exp(log_beta + log_stick), 0.0)
        outs.append(jnp.einsum('bhqk,bhkd->bhqd', w.astype(v.dtype), vb))
    o = jnp.concatenate(outs, axis=2)
    o = o.transpose(0, 2, 1, 3).reshape(bsz, seq, SB_HEADS * SB_HEAD_DIM)
    return o @ w_o


def setup_inputs(seed: int = 0) -> dict:
    key = jax.random.key(seed)
    ks = iter(jax.random.split(key, 40))
    f32 = jnp.float32
    D, F = D_MODEL, D_FF
    G, H, P = S5_GROUPS, S5_GROUP, S5_STATE

    def nrm(shape, scale):
        return jax.random.normal(next(ks), shape, f32) * scale

    def gain(shape):
        return jnp.ones(shape, f32) + 0.02 * jax.random.normal(next(ks), shape, f32)

    x = jax.random.normal(next(ks), (BATCH, SEQ, D), f32)
    norm_ffn1 = gain((DEPTH, D))
    ffn1_w_gate = nrm((DEPTH, D, F), D ** -0.5)
    ffn1_w_up = nrm((DEPTH, D, F), D ** -0.5)
    ffn1_w_down = nrm((DEPTH, F, D), F ** -0.5)
    norm_mix = gain((DEPTH, D))
    s5_w_in = nrm((N_A, D, G * H), D ** -0.5)
    s5_lam_re = -0.5 + 0.01 * jax.random.normal(next(ks), (N_A, G, P), f32)
    s5_lam_im = math.pi * jnp.arange(P, dtype=f32)[None, None, :] + 0.01 * jax.random.normal(next(ks), (N_A, G, P), f32)
    s5_log_dt = jax.random.uniform(next(ks), (N_A, G), f32, math.log(S5_DT_MIN), math.log(S5_DT_MAX))
    s5_b_re = nrm((N_A, G, P, H), (2 * H) ** -0.5)
    s5_b_im = nrm((N_A, G, P, H), (2 * H) ** -0.5)
    s5_c_re = nrm((N_A, G, H, P), P ** -0.5)
    s5_c_im = nrm((N_A, G, H, P), P ** -0.5)
    s5_d = nrm((N_A, G * H), 1.0)
    s5_w_glu = nrm((N_A, G * H, G * H), (G * H) ** -0.5)
    s5_b_glu = nrm((N_A, G * H), 0.01)
    s5_w_out = nrm((N_A, G * H, D), (G * H) ** -0.5)
    sb_w_qkv = nrm((N_B, D, 3 * SB_HEADS * SB_HEAD_DIM), D ** -0.5)
    sb_g_q = gain((N_B, SB_HEAD_DIM))
    sb_g_k = gain((N_B, SB_HEAD_DIM))
    sb_w_o = nrm((N_B, SB_HEADS * SB_HEAD_DIM, D), (SB_HEADS * SB_HEAD_DIM) ** -0.5)
    norm_ffn2 = gain((DEPTH, D))
    ffn2_w_gate = nrm((DEPTH, D, F), D ** -0.5)
    ffn2_w_up = nrm((DEPTH, D, F), D ** -0.5)
    ffn2_w_down = nrm((DEPTH, F, D), F ** -0.5)
    return {
        "x": x,
        "norm_ffn1": norm_ffn1, "ffn1_w_gate": ffn1_w_gate, "ffn1_w_up": ffn1_w_up, "ffn1_w_down": ffn1_w_down,
        "norm_mix": norm_mix,
        "s5_w_in": s5_w_in, "s5_lam_re": s5_lam_re, "s5_lam_im": s5_lam_im, "s5_log_dt": s5_log_dt,
        "s5_b_re": s5_b_re, "s5_b_im": s5_b_im, "s5_c_re": s5_c_re, "s5_c_im": s5_c_im, "s5_d": s5_d,
        "s5_w_glu": s5_w_glu, "s5_b_glu": s5_b_glu, "s5_w_out": s5_w_out,
        "sb_w_qkv": sb_w_qkv, "sb_g_q": sb_g_q, "sb_g_k": sb_g_k, "sb_w_o": sb_w_o,
        "norm_ffn2": norm_ffn2, "ffn2_w_gate": ffn2_w_gate, "ffn2_w_up": ffn2_w_up, "ffn2_w_down": ffn2_w_down,
    }


def reference(x, norm_ffn1, ffn1_w_gate, ffn1_w_up, ffn1_w_down, norm_mix,
              s5_w_in, s5_lam_re, s5_lam_im, s5_log_dt, s5_b_re, s5_b_im, s5_c_re, s5_c_im, s5_d,
              s5_w_glu, s5_b_glu, s5_w_out,
              sb_w_qkv, sb_g_q, sb_g_k, sb_w_o,
              norm_ffn2, ffn2_w_gate, ffn2_w_up, ffn2_w_down):
    for i in range(DEPTH):
        h = rmsnorm(x, norm_ffn1[i])
        x = x + FFN_RES * swiglu(h, ffn1_w_gate[i], ffn1_w_up[i], ffn1_w_down[i])
        h = rmsnorm(x, norm_mix[i])
        j = i // N_MIXERS
        if i % N_MIXERS == 0:
            x = x + s5_mixer(h, s5_w_in[j], s5_lam_re[j], s5_lam_im[j], s5_log_dt[j],
                             s5_b_re[j], s5_b_im[j], s5_c_re[j], s5_c_im[j], s5_d[j],
                             s5_w_glu[j], s5_b_glu[j], s5_w_out[j])
        else:
            x = x + stick_breaking_mixer(h, sb_w_qkv[j], sb_g_q[j], sb_g_k[j], sb_w_o[j])
        h = rmsnorm(x, norm_ffn2[i])
        x = x + FFN_RES * swiglu(h, ffn2_w_gate[i], ffn2_w_up[i], ffn2_w_down[i])
    return x
```

```python
import functools
import math

import jax
import jax.numpy as jnp
from jax import lax
from jax.experimental import pallas as pl
from jax.experimental.pallas import tpu as pltpu

EPS = 1e-6
FFN_RES = 0.5
S5_LAM_RE_MAX = -1e-4
LANES = 128
SUBLANES = 8
VMEM_LIMIT = 56 * 1024 * 1024
S5_GROUPS_PER_BLOCK = 16

f32 = jnp.float32
bf16 = jnp.bfloat16


def _params(*sem):
    return pltpu.CompilerParams(dimension_semantics=sem, vmem_limit_bytes=VMEM_LIMIT)


def _rmsnorm(x, g):
    ms = jnp.mean(x * x, axis=-1, keepdims=True)
    return x * lax.rsqrt(ms + EPS) * g


def _ffn_kernel(x_ref, g_ref, wg_ref, wu_ref, wd_ref, o_ref, h_ref):
    @pl.when(pl.program_id(1) == 0)
    def _():
        x = x_ref[...]
        h_ref[...] = _rmsnorm(x, g_ref[...]).astype(bf16)
        o_ref[...] = x

    h = h_ref[...]
    gate = jnp.dot(h, wg_ref[...], preferred_element_type=f32)
    up = jnp.dot(h, wu_ref[...], preferred_element_type=f32)
    a = (gate * jax.nn.sigmoid(gate) * up * FFN_RES).astype(bf16)
    o_ref[...] += jnp.dot(a, wd_ref[...], preferred_element_type=f32)


def _ffn(x, g, wg, wu, wd, *, tm=512, tf=512):
    n, d = x.shape
    f = wg.shape[1]
    tm, tf = min(tm, n), min(tf, f)
    assert n % tm == 0 and f % tf == 0
    return pl.pallas_call(
        _ffn_kernel,
        out_shape=jax.ShapeDtypeStruct((n, d), f32),
        grid=(n // tm, f // tf),
        in_specs=[
            pl.BlockSpec((tm, d), lambda i, j: (i, 0)),
            pl.BlockSpec((1, d), lambda i, j: (0, 0)),
            pl.BlockSpec((d, tf), lambda i, j: (0, j)),
            pl.BlockSpec((d, tf), lambda i, j: (0, j)),
            pl.BlockSpec((tf, d), lambda i, j: (j, 0)),
        ],
        out_specs=pl.BlockSpec((tm, d), lambda i, j: (i, 0)),
        scratch_shapes=[pltpu.VMEM((tm, d), bf16)],
        compiler_params=_params("parallel", "arbitrary"),
        name="ffn",
    )(x, g.reshape(1, d), wg, wu, wd)


def _norm_matmul_kernel(x_ref, g_ref, w_ref, o_ref, h_ref):
    @pl.when(pl.program_id(1) == 0)
    def _():
        h_ref[...] = _rmsnorm(x_ref[...], g_ref[...]).astype(bf16)

    o_ref[...] = jnp.dot(h_ref[...], w_ref[...], preferred_element_type=f32).astype(o_ref.dtype)


def _norm_matmul(x, g, w, *, out_dtype, tm=512, tn=1024):
    n, d = x.shape
    dout = w.shape[1]
    tm, tn = min(tm, n), min(tn, dout)
    assert n % tm == 0 and dout % tn == 0
    return pl.pallas_call(
        _norm_matmul_kernel,
        out_shape=jax.ShapeDtypeStruct((n, dout), out_dtype),
        grid=(n // tm, dout // tn),
        in_specs=[
            pl.BlockSpec((tm, d), lambda i, j: (i, 0)),
            pl.BlockSpec((1, d), lambda i, j: (0, 0)),
            pl.BlockSpec((d, tn), lambda i, j: (0, j)),
        ],
        out_specs=pl.BlockSpec((tm, tn), lambda i, j: (i, j)),
        scratch_shapes=[pltpu.VMEM((tm, d), bf16)],
        compiler_params=_params("parallel", "arbitrary"),
        name="norm_matmul",
    )(x, g.reshape(1, d), w)


def _qkv_kernel(x_ref, g_ref, w_ref, hg_ref, o_ref, h_ref, *, n_norm_tiles, head_dim):
    j = pl.program_id(1)

    @pl.when(j == 0)
    def _():
        h_ref[...] = _rmsnorm(x_ref[...], g_ref[...]).astype(bf16)

    acc = jnp.dot(h_ref[...], w_ref[...], preferred_element_type=f32)

    @pl.when(j < n_norm_tiles)
    def _():
        hg = hg_ref[...]
        for c in range(0, acc.shape[1], head_dim):
            o_ref[:, c:c + head_dim] = _rmsnorm(acc[:, c:c + head_dim], hg[:, c:c + head_dim]).astype(o_ref.dtype)

    @pl.when(j >= n_norm_tiles)
    def _():
        o_ref[...] = acc.astype(o_ref.dtype)


def _qkv(x, g, w, head_gain, n_norm_cols, head_dim, *, tm=512, tn=512):
    n, d = x.shape
    dout = w.shape[1]
    tm, tn = min(tm, n), min(tn, dout)
    assert n % tm == 0 and dout % tn == 0 and n_norm_cols % tn == 0 and tn % head_dim == 0
    n_norm_tiles = n_norm_cols // tn
    return pl.pallas_call(
        functools.partial(_qkv_kernel, n_norm_tiles=n_norm_tiles, head_dim=head_dim),
        out_shape=jax.ShapeDtypeStruct((n, dout), bf16),
        grid=(n // tm, dout // tn),
        in_specs=[
            pl.BlockSpec((tm, d), lambda i, j: (i, 0)),
            pl.BlockSpec((1, d), lambda i, j: (0, 0)),
            pl.BlockSpec((d, tn), lambda i, j: (0, j)),
            pl.BlockSpec((1, tn), lambda i, j: (0, jnp.minimum(j, n_norm_tiles - 1))),
        ],
        out_specs=pl.BlockSpec((tm, tn), lambda i, j: (i, j)),
        scratch_shapes=[pltpu.VMEM((tm, d), bf16)],
        compiler_params=_params("parallel", "arbitrary"),
        name="qkv",
    )(x, g.reshape(1, d), w, head_gain)


def _matmul_res_kernel(a_ref, w_ref, x_ref, o_ref):
    o_ref[...] = x_ref[...] + jnp.dot(a_ref[...], w_ref[...], preferred_element_type=f32)


def _matmul_res(a, w, x, *, tm=512, tn=1024):
    n, k = a.shape
    d = w.shape[1]
    tm, tn = min(tm, n), min(tn, d)
    assert n % tm == 0 and d % tn == 0
    return pl.pallas_call(
        _matmul_res_kernel,
        out_shape=jax.ShapeDtypeStruct((n, d), f32),
        grid=(n // tm, d // tn),
        in_specs=[
            pl.BlockSpec((tm, k), lambda i, j: (i, 0)),
            pl.BlockSpec((k, tn), lambda i, j: (0, j)),
            pl.BlockSpec((tm, tn), lambda i, j: (i, j)),
        ],
        out_specs=pl.BlockSpec((tm, tn), lambda i, j: (i, j)),
        compiler_params=_params("parallel", "arbitrary"),
        name="matmul_res",
    )(a, w, x)


def _glu_out_kernel(y_ref, yj_ref, x_ref, wg_ref, b_ref, wo_ref, o_ref, yb_ref):
    @pl.when(pl.program_id(1) == 0)
    def _():
        yb_ref[...] = y_ref[...].astype(bf16)
        o_ref[...] = x_ref[...]

    z = jnp.dot(yb_ref[...], wg_ref[...], preferred_element_type=f32) + b_ref[...]
    a = (yj_ref[...] * jax.nn.sigmoid(z)).astype(bf16)
    o_ref[...] += jnp.dot(a, wo_ref[...], preferred_element_type=f32)


def _glu_out(y, x, w_glu, b_glu, w_out, *, tm=512, tn=512):
    n, c = y.shape
    d = w_out.shape[1]
    tm, tn = min(tm, n), min(tn, c)
    assert n % tm == 0 and c % tn == 0
    return pl.pallas_call(
        _glu_out_kernel,
        out_shape=jax.ShapeDtypeStruct((n, d), f32),
        grid=(n // tm, c // tn),
        in_specs=[
            pl.BlockSpec((tm, c), lambda i, j: (i, 0)),
            pl.BlockSpec((tm, tn), lambda i, j: (i, j)),
            pl.BlockSpec((tm, d), lambda i, j: (i, 0)),
            pl.BlockSpec((c, tn), lambda i, j: (0, j)),
            pl.BlockSpec((1, tn), lambda i, j: (0, j)),
            pl.BlockSpec((tn, d), lambda i, j: (j, 0)),
        ],
        out_specs=pl.BlockSpec((tm, d), lambda i, j: (i, 0)),
        scratch_shapes=[pltpu.VMEM((tm, c), bf16)],
        compiler_params=_params("parallel", "arbitrary"),
        name="glu_out",
    )(y, y, x, w_glu, b_glu.reshape(1, c), w_out)


def _s5_kernel(u_ref, bw_ref, cw_ref, lam_ref, d_ref, o_ref, s_ref, carry_ref, *, half):
    tc = u_ref.shape[0]

    @pl.when(pl.program_id(2) == 0)
    def _():
        carry_ref[...] = jnp.zeros_like(carry_ref)

    u = u_ref[...]
    s_ref[...] = jnp.dot(u.astype(bf16), bw_ref[0], preferred_element_type=f32)

    for c in range(half // LANES):
        re_l = pl.ds(c * LANES, LANES)
        im_l = pl.ds(half + c * LANES, LANES)
        lam = [(lam_ref[0, 2 * k, :, re_l], lam_ref[0, 2 * k + 1, :, re_l]) for k in range(4)]

        def time_tile(i, carry, re_l=re_l, im_l=im_l, lam=lam):
            cre, cim = carry
            rows = pl.ds(pl.multiple_of(i * SUBLANES, SUBLANES), SUBLANES)
            xre = s_ref[rows, re_l]
            xim = s_ref[rows, im_l]
            for k, dist in enumerate((1, 2, 4)):
                lre, lim = lam[k]
                pre = pltpu.roll(xre, dist, 0)
                pim = pltpu.roll(xim, dist, 0)
                xre, xim = xre + (lre * pre - lim * pim), xim + (lre * pim + lim * pre)
            are, aim = lam[3]
            xre = xre + (are * cre - aim * cim)
            xim = xim + (are * cim + aim * cre)
            s_ref[rows, re_l] = xre
            s_ref[rows, im_l] = xim
            last = SUBLANES - 1
            return (jnp.broadcast_to(xre[last:last + 1, :], xre.shape),
                    jnp.broadcast_to(xim[last:last + 1, :], xim.shape))

        cre, cim = lax.fori_loop(0, tc // SUBLANES, time_tile, (carry_ref[:, re_l], carry_ref[:, im_l]))
        carry_ref[:, re_l] = cre
        carry_ref[:, im_l] = cim

    y = jnp.dot(s_ref[...].astype(bf16), cw_ref[0], preferred_element_type=f32) + d_ref[...] * u
    o_ref[...] = jax.nn.gelu(y)


def _s5_discretise(lam_re, lam_im, log_dt, b_re, b_im):
    dt = jnp.exp(log_dt.astype(f32))[:, None]
    lr = jnp.minimum(lam_re.astype(f32), S5_LAM_RE_MAX)
    li = lam_im.astype(f32)
    mag = jnp.exp(lr * dt)
    ab_re = mag * jnp.cos(li * dt)
    ab_im = mag * jnp.sin(li * dt)
    den = lr * lr + li * li
    n_re = ab_re - 1.0
    f_re = (n_re * lr + ab_im * li) / den
    f_im = (ab_im * lr - n_re * li) / den
    br = b_re.astype(f32)
    bi = b_im.astype(f32)
    bb_re = f_re[..., None] * br - f_im[..., None] * bi
    bb_im = f_re[..., None] * bi + f_im[..., None] * br
    return lr * dt, li * dt, bb_re, bb_im


def _block_diag(w, gpb):
    g, a, b = w.shape
    w = w.reshape(g // gpb, gpb, a, b)
    eye = jnp.eye(gpb, dtype=w.dtype)
    return jnp.einsum("ngab,gh->ngahb", w, eye).reshape(g // gpb, gpb * a, gpb * b)


def _s5_scan_consts(z_re, z_im, gpb):
    g, p = z_re.shape

    def power(k):
        m = jnp.exp(z_re * k)
        return m * jnp.cos(z_im * k), m * jnp.sin(z_im * k)

    sub = jnp.arange(SUBLANES)[:, None, None]
    rows = []
    for dist in (1, 2, 4):
        pr, pi = power(float(dist))
        rows.append(jnp.where(sub >= dist, pr[None], 0.0))
        rows.append(jnp.where(sub >= dist, pi[None], 0.0))
    ks = (jnp.arange(SUBLANES, dtype=f32) + 1.0)[:, None, None]
    m = jnp.exp(z_re[None] * ks)
    rows.append(m * jnp.cos(z_im[None] * ks))
    rows.append(m * jnp.sin(z_im[None] * ks))
    c = jnp.stack(rows)
    c = c.reshape(8, SUBLANES, g // gpb, gpb * p)
    return c.transpose(2, 0, 1, 3)


def _s5_scan(u, lam_re, lam_im, log_dt, b_re, b_im, c_re, c_im, d_skip, *, batch, tc=256):
    n, ch = u.shape
    g, p, h = b_re.shape
    seq = n // batch
    gpb = min(S5_GROUPS_PER_BLOCK, g)
    tc = min(tc, seq)
    assert g % gpb == 0 and seq % tc == 0 and tc % SUBLANES == 0 and (gpb * p) % LANES == 0
    nb = g // gpb
    half = gpb * p
    z_re, z_im, bb_re, bb_im = _s5_discretise(lam_re, lam_im, log_dt, b_re, b_im)
    bw = jnp.concatenate([_block_diag(bb_re.transpose(0, 2, 1), gpb),
                          _block_diag(bb_im.transpose(0, 2, 1), gpb)], axis=2).astype(bf16)
    cw = jnp.concatenate([_block_diag(c_re.astype(f32).transpose(0, 2, 1), gpb),
                          _block_diag(-c_im.astype(f32).transpose(0, 2, 1), gpb)], axis=1).astype(bf16)
    lam = _s5_scan_consts(z_re, z_im, gpb)
    cb = gpb * h
    nt = seq // tc
    return pl.pallas_call(
        functools.partial(_s5_kernel, half=half),
        out_shape=jax.ShapeDtypeStruct((n, ch), f32),
        grid=(batch, nb, nt),
        in_specs=[
            pl.BlockSpec((tc, cb), lambda b, j, t: (b * nt + t, j)),
            pl.BlockSpec((1, cb, 2 * half), lambda b, j, t: (j, 0, 0)),
            pl.BlockSpec((1, 2 * half, cb), lambda b, j, t: (j, 0, 0)),
            pl.BlockSpec((1, 8, SUBLANES, half), lambda b, j, t: (j, 0, 0, 0)),
            pl.BlockSpec((1, cb), lambda b, j, t: (0, j)),
        ],
        out_specs=pl.BlockSpec((tc, cb), lambda b, j, t: (b * nt + t, j)),
        scratch_shapes=[pltpu.VMEM((tc, 2 * half), f32), pltpu.VMEM((SUBLANES, 2 * half), f32)],
        compiler_params=_params("parallel", "parallel", "arbitrary"),
        name="s5_scan",
    )(u, bw, cw, lam, d_skip.reshape(1, ch).astype(f32))


def _sb_kernel(q_ref, k_ref, v_ref, m_ref, o_ref, *, scale):
    t = q_ref.shape[0]
    qi = pl.program_id(2)
    q = q_ref[...]
    m = m_ref[...]
    row = lax.broadcasted_iota(jnp.int32, (t, t), 0)
    col = lax.broadcasted_iota(jnp.int32, (t, t), 1)
    past = col < row

    def block(kb, carry, acc, masked):
        ks = pl.ds(pl.multiple_of(kb * t, t), t)
        z = lax.dot_general(q, k_ref[ks, :], (((1,), (1,)), ((), ())), preferred_element_type=f32) * scale
        soft = jnp.log1p(jnp.exp(-jnp.abs(z)))
        log_beta = jnp.minimum(z, 0.0) - soft
        log_keep = -jnp.maximum(z, 0.0) - soft
        if masked:
            log_keep = jnp.where(past, log_keep, 0.0)
        hi = log_keep.astype(bf16)
        lo = (log_keep - hi.astype(f32)).astype(bf16)
        cs = jnp.dot(jnp.concatenate([hi, lo], axis=1), m, preferred_element_type=f32)
        log_stick = cs[:, :t] + jnp.concatenate([carry] * (t // LANES), axis=1)
        w = jnp.exp(log_beta + log_stick)
        if masked:
            w = jnp.where(past, w, 0.0)
        acc = acc + jnp.dot(w.astype(bf16), v_ref[ks, :], preferred_element_type=f32)
        return carry + cs[:, t:], acc

    carry0 = jnp.zeros((t, LANES), f32)
    acc0 = jnp.zeros((t, q.shape[1]), f32)
    carry, acc = block(qi, carry0, acc0, True)

    def body(i, ca):
        return block(qi - 1 - i, ca[0], ca[1], False)

    carry, acc = lax.fori_loop(0, qi, body, (carry, acc))
    o_ref[...] = acc.astype(o_ref.dtype)


def _sb_attention(qkv, *, batch, heads, head_dim, t=128):
    n = qkv.shape[0]
    seq = n // batch
    t = min(t, seq)
    assert seq % t == 0 and t % LANES == 0 and head_dim == LANES
    nq = seq // t
    tri = (jnp.arange(t)[:, None] > jnp.arange(t)[None, :]).astype(bf16)
    m = jnp.concatenate([tri, jnp.ones((t, LANES), bf16)], axis=1)
    m = jnp.concatenate([m, m], axis=0)
    return pl.pallas_call(
        functools.partial(_sb_kernel, scale=1.0 / math.sqrt(head_dim)),
        out_shape=jax.ShapeDtypeStruct((n, heads * head_dim), bf16),
        grid=(batch, heads, nq),
        in_specs=[
            pl.BlockSpec((t, head_dim), lambda b, h, i: (b * nq + i, h)),
            pl.BlockSpec((seq, head_dim), lambda b, h, i: (b, heads + h)),
            pl.BlockSpec((seq, head_dim), lambda b, h, i: (b, 2 * heads + h)),
            pl.BlockSpec((2 * t, t + LANES), lambda b, h, i: (0, 0)),
        ],
        out_specs=pl.BlockSpec((t, head_dim), lambda b, h, i: (b * nq + i, h)),
        compiler_params=_params("parallel", "parallel", "arbitrary"),
        name="sb_attention",
    )(qkv, qkv, qkv, m)


def kernel(x, norm_ffn1, ffn1_w_gate, ffn1_w_up, ffn1_w_down, norm_mix, s5_w_in, s5_lam_re, s5_lam_im, s5_log_dt, s5_b_re, s5_b_im, s5_c_re, s5_c_im, s5_d, s5_w_glu, s5_b_glu, s5_w_out, sb_w_qkv, sb_g_q, sb_g_k, sb_w_o, norm_ffn2, ffn2_w_gate, ffn2_w_up, ffn2_w_down):
    batch, seq, d = x.shape
    depth = norm_ffn1.shape[0]
    head_dim = sb_g_q.shape[1]
    heads = sb_w_o.shape[1] // head_dim
    n_mixers = 2
    xf = x.reshape(batch * seq, d).astype(f32)
    for i in range(depth):
        xf = _ffn(xf, norm_ffn1[i], ffn1_w_gate[i].astype(bf16), ffn1_w_up[i].astype(bf16),
                  ffn1_w_down[i].astype(bf16))
        j = i // n_mixers
        if i % n_mixers == 0:
            u = _norm_matmul(xf, norm_mix[i], s5_w_in[j].astype(bf16), out_dtype=f32)
            y = _s5_scan(u, s5_lam_re[j], s5_lam_im[j], s5_log_dt[j], s5_b_re[j], s5_b_im[j],
                         s5_c_re[j], s5_c_im[j], s5_d[j], batch=batch)
            xf = _glu_out(y, xf, s5_w_glu[j].astype(bf16), s5_b_glu[j].astype(f32), s5_w_out[j].astype(bf16))
        else:
            hd = heads * head_dim
            head_gain = jnp.concatenate([jnp.tile(sb_g_q[j].astype(f32), heads),
                                         jnp.tile(sb_g_k[j].astype(f32), heads)]).reshape(1, 2 * hd)
            qkv = _qkv(xf, norm_mix[i], sb_w_qkv[j].astype(bf16), head_gain, 2 * hd, head_dim)
            o = _sb_attention(qkv, batch=batch, heads=heads, head_dim=head_dim)
            xf = _matmul_res(o, sb_w_o[j].astype(bf16), xf)
        xf = _ffn(xf, norm_ffn2[i], ffn2_w_gate[i].astype(bf16), ffn2_w_up[i].astype(bf16),
                  ffn2_w_down[i].astype(bf16))
    return xf.reshape(batch, seq, d).astype(x.dtype)
```

```python
import functools
import math

import jax
import jax.numpy as jnp
from jax import lax
from jax.experimental import pallas as pl
from jax.experimental.pallas import tpu as pltpu

EPS = 1e-6
FFN_RES = 0.5
S5_LAM_RE_MAX = -1e-4
LOG2E = 1.4426950408889634
LANES = 128
SUBLANES = 8
VMEM_LIMIT = 56 * 1024 * 1024
S5_GROUPS_PER_BLOCK = 16

f32 = jnp.float32
bf16 = jnp.bfloat16


def _params(*sem):
    return pltpu.CompilerParams(dimension_semantics=sem, vmem_limit_bytes=VMEM_LIMIT)


def _rmsnorm(x, g):
    ms = jnp.mean(x * x, axis=-1, keepdims=True)
    return x * lax.rsqrt(ms + EPS) * g


def _ffn_kernel(x_ref, g_ref, wg_ref, wu_ref, wd_ref, o_ref, h_ref):
    @pl.when(pl.program_id(1) == 0)
    def _():
        x = x_ref[...]
        h_ref[...] = _rmsnorm(x, g_ref[...]).astype(bf16)
        o_ref[...] = x

    h = h_ref[...]
    gate = jnp.dot(h, wg_ref[...], preferred_element_type=f32)
    up = jnp.dot(h, wu_ref[...], preferred_element_type=f32)
    a = (gate * jax.nn.sigmoid(gate) * up * FFN_RES).astype(bf16)
    o_ref[...] += jnp.dot(a, wd_ref[...], preferred_element_type=f32)


def _ffn(x, g, wg, wu, wd, *, tm=512, tf=512):
    n, d = x.shape
    f = wg.shape[1]
    tm, tf = min(tm, n), min(tf, f)
    assert n % tm == 0 and f % tf == 0
    return pl.pallas_call(
        _ffn_kernel,
        out_shape=jax.ShapeDtypeStruct((n, d), f32),
        grid=(n // tm, f // tf),
        in_specs=[
            pl.BlockSpec((tm, d), lambda i, j: (i, 0)),
            pl.BlockSpec((1, d), lambda i, j: (0, 0)),
            pl.BlockSpec((d, tf), lambda i, j: (0, j)),
            pl.BlockSpec((d, tf), lambda i, j: (0, j)),
            pl.BlockSpec((tf, d), lambda i, j: (j, 0)),
        ],
        out_specs=pl.BlockSpec((tm, d), lambda i, j: (i, 0)),
        scratch_shapes=[pltpu.VMEM((tm, d), bf16)],
        compiler_params=_params("parallel", "arbitrary"),
        name="ffn",
    )(x, g.reshape(1, d), wg, wu, wd)


def _norm_matmul_kernel(x_ref, g_ref, w_ref, o_ref, h_ref):
    @pl.when(pl.program_id(1) == 0)
    def _():
        h_ref[...] = _rmsnorm(x_ref[...], g_ref[...]).astype(bf16)

    o_ref[...] = jnp.dot(h_ref[...], w_ref[...], preferred_element_type=f32).astype(o_ref.dtype)


def _norm_matmul(x, g, w, *, out_dtype, tm=512, tn=2048):
    n, d = x.shape
    dout = w.shape[1]
    tm, tn = min(tm, n), min(tn, dout)
    assert n % tm == 0 and dout % tn == 0
    return pl.pallas_call(
        _norm_matmul_kernel,
        out_shape=jax.ShapeDtypeStruct((n, dout), out_dtype),
        grid=(n // tm, dout // tn),
        in_specs=[
            pl.BlockSpec((tm, d), lambda i, j: (i, 0)),
            pl.BlockSpec((1, d), lambda i, j: (0, 0)),
            pl.BlockSpec((d, tn), lambda i, j: (0, j)),
        ],
        out_specs=pl.BlockSpec((tm, tn), lambda i, j: (i, j)),
        scratch_shapes=[pltpu.VMEM((tm, d), bf16)],
        compiler_params=_params("parallel", "arbitrary"),
        name="norm_matmul",
    )(x, g.reshape(1, d), w)


def _qkv_kernel(x_ref, g_ref, w_ref, hg_ref, o_ref, h_ref, *, n_norm_tiles, head_dim):
    j = pl.program_id(1)

    @pl.when(j == 0)
    def _():
        h_ref[...] = _rmsnorm(x_ref[...], g_ref[...]).astype(bf16)

    acc = jnp.dot(h_ref[...], w_ref[...], preferred_element_type=f32)

    @pl.when(j < n_norm_tiles)
    def _():
        hg = hg_ref[...]
        for c in range(0, acc.shape[1], head_dim):
            o_ref[:, c:c + head_dim] = _rmsnorm(acc[:, c:c + head_dim], hg[:, c:c + head_dim]).astype(o_ref.dtype)

    @pl.when(j >= n_norm_tiles)
    def _():
        o_ref[...] = acc.astype(o_ref.dtype)


def _qkv(x, g, w, head_gain, n_norm_cols, head_dim, *, tm=1024, tn=512):
    n, d = x.shape
    dout = w.shape[1]
    tm, tn = min(tm, n), min(tn, dout)
    assert n % tm == 0 and dout % tn == 0 and n_norm_cols % tn == 0 and tn % head_dim == 0
    n_norm_tiles = n_norm_cols // tn
    return pl.pallas_call(
        functools.partial(_qkv_kernel, n_norm_tiles=n_norm_tiles, head_dim=head_dim),
        out_shape=jax.ShapeDtypeStruct((n, dout), bf16),
        grid=(n // tm, dout // tn),
        in_specs=[
            pl.BlockSpec((tm, d), lambda i, j: (i, 0)),
            pl.BlockSpec((1, d), lambda i, j: (0, 0)),
            pl.BlockSpec((d, tn), lambda i, j: (0, j)),
            pl.BlockSpec((1, tn), lambda i, j: (0, jnp.minimum(j, n_norm_tiles - 1))),
        ],
        out_specs=pl.BlockSpec((tm, tn), lambda i, j: (i, j)),
        scratch_shapes=[pltpu.VMEM((tm, d), bf16)],
        compiler_params=_params("parallel", "arbitrary"),
        name="qkv",
    )(x, g.reshape(1, d), w, head_gain)


def _matmul_res_kernel(a_ref, w_ref, x_ref, o_ref):
    o_ref[...] = x_ref[...] + jnp.dot(a_ref[...], w_ref[...], preferred_element_type=f32)


def _matmul_res(a, w, x, *, tm=512, tn=2048):
    n, k = a.shape
    d = w.shape[1]
    tm, tn = min(tm, n), min(tn, d)
    assert n % tm == 0 and d % tn == 0
    return pl.pallas_call(
        _matmul_res_kernel,
        out_shape=jax.ShapeDtypeStruct((n, d), f32),
        grid=(n // tm, d // tn),
        in_specs=[
            pl.BlockSpec((tm, k), lambda i, j: (i, 0)),
            pl.BlockSpec((k, tn), lambda i, j: (0, j)),
            pl.BlockSpec((tm, tn), lambda i, j: (i, j)),
        ],
        out_specs=pl.BlockSpec((tm, tn), lambda i, j: (i, j)),
        compiler_params=_params("parallel", "arbitrary"),
        name="matmul_res",
    )(a, w, x)


def _glu_out_kernel(y_ref, yj_ref, x_ref, wg_ref, b_ref, wo_ref, o_ref, yb_ref):
    @pl.when(pl.program_id(1) == 0)
    def _():
        yb_ref[...] = y_ref[...].astype(bf16)
        o_ref[...] = x_ref[...]

    z = jnp.dot(yb_ref[...], wg_ref[...], preferred_element_type=f32) + b_ref[...]
    a = (yj_ref[...] * jax.nn.sigmoid(z)).astype(bf16)
    o_ref[...] += jnp.dot(a, wo_ref[...], preferred_element_type=f32)


def _glu_out(y, x, w_glu, b_glu, w_out, *, tm=512, tn=512):
    n, c = y.shape
    d = w_out.shape[1]
    tm, tn = min(tm, n), min(tn, c)
    assert n % tm == 0 and c % tn == 0
    return pl.pallas_call(
        _glu_out_kernel,
        out_shape=jax.ShapeDtypeStruct((n, d), f32),
        grid=(n // tm, c // tn),
        in_specs=[
            pl.BlockSpec((tm, c), lambda i, j: (i, 0)),
            pl.BlockSpec((tm, tn), lambda i, j: (i, j)),
            pl.BlockSpec((tm, d), lambda i, j: (i, 0)),
            pl.BlockSpec((c, tn), lambda i, j: (0, j)),
            pl.BlockSpec((1, tn), lambda i, j: (0, j)),
            pl.BlockSpec((tn, d), lambda i, j: (j, 0)),
        ],
        out_specs=pl.BlockSpec((tm, d), lambda i, j: (i, 0)),
        scratch_shapes=[pltpu.VMEM((tm, c), bf16)],
        compiler_params=_params("parallel", "arbitrary"),
        name="glu_out",
    )(y, y, x, w_glu, b_glu.reshape(1, c), w_out)


def _s5_kernel(u_ref, bw_ref, cw_ref, lam_ref, d_ref, o_ref, s_ref, carry_ref, *, half):
    tc = u_ref.shape[0]

    @pl.when(pl.program_id(2) == 0)
    def _():
        carry_ref[...] = jnp.zeros_like(carry_ref)

    u = u_ref[...]
    s_ref[...] = jnp.dot(u.astype(bf16), bw_ref[0], preferred_element_type=f32)

    n_lane_tiles = half // LANES
    last = SUBLANES - 1

    def time_tile(i, carry):
        rows = pl.ds(pl.multiple_of(i * SUBLANES, SUBLANES), SUBLANES)
        out = []
        for c in range(n_lane_tiles):
            re_l = pl.ds(c * LANES, LANES)
            im_l = pl.ds(half + c * LANES, LANES)
            cre, cim = carry[2 * c], carry[2 * c + 1]
            xre = s_ref[rows, re_l]
            xim = s_ref[rows, im_l]
            for k, dist in enumerate((1, 2, 4)):
                lre = lam_ref[0, 2 * k, :, re_l]
                lim = lam_ref[0, 2 * k + 1, :, re_l]
                pre = pltpu.roll(xre, dist, 0)
                pim = pltpu.roll(xim, dist, 0)
                xre, xim = xre + (lre * pre - lim * pim), xim + (lre * pim + lim * pre)
            are = lam_ref[0, 6, :, re_l]
            aim = lam_ref[0, 7, :, re_l]
            xre = xre + (are * cre - aim * cim)
            xim = xim + (are * cim + aim * cre)
            s_ref[rows, re_l] = xre
            s_ref[rows, im_l] = xim
            out.append(jnp.broadcast_to(xre[last:last + 1, :], xre.shape))
            out.append(jnp.broadcast_to(xim[last:last + 1, :], xim.shape))
        return tuple(out)

    carry0 = []
    for c in range(n_lane_tiles):
        carry0.append(carry_ref[:, pl.ds(c * LANES, LANES)])
        carry0.append(carry_ref[:, pl.ds(half + c * LANES, LANES)])
    carry = lax.fori_loop(0, tc // SUBLANES, time_tile, tuple(carry0))
    for c in range(n_lane_tiles):
        carry_ref[:, pl.ds(c * LANES, LANES)] = carry[2 * c]
        carry_ref[:, pl.ds(half + c * LANES, LANES)] = carry[2 * c + 1]

    y = jnp.dot(s_ref[...].astype(bf16), cw_ref[0], preferred_element_type=f32) + d_ref[...] * u
    o_ref[...] = jax.nn.gelu(y)


def _s5_discretise(lam_re, lam_im, log_dt, b_re, b_im):
    dt = jnp.exp(log_dt.astype(f32))[:, None]
    lr = jnp.minimum(lam_re.astype(f32), S5_LAM_RE_MAX)
    li = lam_im.astype(f32)
    mag = jnp.exp(lr * dt)
    ab_re = mag * jnp.cos(li * dt)
    ab_im = mag * jnp.sin(li * dt)
    den = lr * lr + li * li
    n_re = ab_re - 1.0
    f_re = (n_re * lr + ab_im * li) / den
    f_im = (ab_im * lr - n_re * li) / den
    br = b_re.astype(f32)
    bi = b_im.astype(f32)
    bb_re = f_re[..., None] * br - f_im[..., None] * bi
    bb_im = f_re[..., None] * bi + f_im[..., None] * br
    return lr * dt, li * dt, bb_re, bb_im


def _block_diag(w, gpb):
    g, a, b = w.shape
    w = w.reshape(g // gpb, gpb, a, b)
    eye = jnp.eye(gpb, dtype=w.dtype)
    return jnp.einsum("ngab,gh->ngahb", w, eye).reshape(g // gpb, gpb * a, gpb * b)


def _s5_scan_consts(z_re, z_im, gpb):
    g, p = z_re.shape

    def power(k):
        m = jnp.exp(z_re * k)
        return m * jnp.cos(z_im * k), m * jnp.sin(z_im * k)

    sub = jnp.arange(SUBLANES)[:, None, None]
    rows = []
    for dist in (1, 2, 4):
        pr, pi = power(float(dist))
        rows.append(jnp.where(sub >= dist, pr[None], 0.0))
        rows.append(jnp.where(sub >= dist, pi[None], 0.0))
    ks = (jnp.arange(SUBLANES, dtype=f32) + 1.0)[:, None, None]
    m = jnp.exp(z_re[None] * ks)
    rows.append(m * jnp.cos(z_im[None] * ks))
    rows.append(m * jnp.sin(z_im[None] * ks))
    c = jnp.stack(rows)
    c = c.reshape(8, SUBLANES, g // gpb, gpb * p)
    return c.transpose(2, 0, 1, 3)


def _s5_scan(u, lam_re, lam_im, log_dt, b_re, b_im, c_re, c_im, d_skip, *, batch, tc=512):
    n, ch = u.shape
    g, p, h = b_re.shape
    seq = n // batch
    gpb = min(S5_GROUPS_PER_BLOCK, g)
    tc = min(tc, seq)
    assert g % gpb == 0 and seq % tc == 0 and tc % SUBLANES == 0 and (gpb * p) % LANES == 0
    nb = g // gpb
    half = gpb * p
    z_re, z_im, bb_re, bb_im = _s5_discretise(lam_re, lam_im, log_dt, b_re, b_im)
    bw = jnp.concatenate([_block_diag(bb_re.transpose(0, 2, 1), gpb),
                          _block_diag(bb_im.transpose(0, 2, 1), gpb)], axis=2).astype(bf16)
    cw = jnp.concatenate([_block_diag(c_re.astype(f32).transpose(0, 2, 1), gpb),
                          _block_diag(-c_im.astype(f32).transpose(0, 2, 1), gpb)], axis=1).astype(bf16)
    lam = _s5_scan_consts(z_re, z_im, gpb)
    cb = gpb * h
    nt = seq // tc
    return pl.pallas_call(
        functools.partial(_s5_kernel, half=half),
        out_shape=jax.ShapeDtypeStruct((n, ch), f32),
        grid=(batch, nb, nt),
        in_specs=[
            pl.BlockSpec((tc, cb), lambda b, j, t: (b * nt + t, j)),
            pl.BlockSpec((1, cb, 2 * half), lambda b, j, t: (j, 0, 0)),
            pl.BlockSpec((1, 2 * half, cb), lambda b, j, t: (j, 0, 0)),
            pl.BlockSpec((1, 8, SUBLANES, half), lambda b, j, t: (j, 0, 0, 0)),
            pl.BlockSpec((1, cb), lambda b, j, t: (0, j)),
        ],
        out_specs=pl.BlockSpec((tc, cb), lambda b, j, t: (b * nt + t, j)),
        scratch_shapes=[pltpu.VMEM((tc, 2 * half), f32), pltpu.VMEM((SUBLANES, 2 * half), f32)],
        compiler_params=_params("parallel", "parallel", "arbitrary"),
        name="s5_scan",
    )(u, bw, cw, lam, d_skip.reshape(1, ch).astype(f32))


def _sb_kernel(q_ref, k_ref, v_ref, m_ref, o_ref, acc_ref, carry_ref, z0_ref, z1_ref, w0_ref, w1_ref,
               *, heads_per_block):
    t = q_ref.shape[0]
    nsub = t // LANES
    qi = pl.program_id(2)
    m = m_ref[...]
    row = lax.broadcasted_iota(jnp.int32, (t, t), 0)
    col = lax.broadcasted_iota(jnp.int32, (t, t), 1)
    past = col < row
    acc_ref[...] = jnp.zeros_like(acc_ref)
    carry_ref[...] = jnp.zeros_like(carry_ref)

    def key_rows(kb):
        return pl.ds(pl.multiple_of(kb * t, t), t)

    def scores(kb, z_ref):
        ks = key_rows(kb)
        for hh in range(heads_per_block):
            z_ref[:, pl.ds(hh * t, t)] = lax.dot_general(
                q_ref[:, pl.ds(hh * LANES, LANES)], k_ref[ks, pl.ds(hh * LANES, LANES)],
                (((1,), (1,)), ((), ())), preferred_element_type=f32)

    def weights(z_ref, w_ref, masked=False):
        for hh in range(heads_per_block):
            hl = pl.ds(hh * LANES, LANES)
            z = z_ref[:, pl.ds(hh * t, t)]
            zneg = jnp.minimum(z, 0.0)
            zdiff = zneg - z
            soft = jnp.log(1.0 + jnp.exp2(zdiff + zneg)) * LOG2E
            log_beta = zneg - soft
            log_keep = zdiff - soft
            if masked:
                log_keep = jnp.where(past, log_keep, 0.0)
            run = carry_ref[:, hl]
            for j in reversed(range(nsub)):
                sl = slice(j * LANES, (j + 1) * LANES)
                lk = log_keep[:, sl]
                hi = lk.astype(bf16)
                lo = (lk - hi.astype(f32)).astype(bf16)
                cs = jnp.dot(jnp.concatenate([hi, lo], axis=1), m, preferred_element_type=f32)
                w = jnp.exp2(log_beta[:, sl] + cs[:, :LANES] + run)
                if masked:
                    w = jnp.where(past[:, sl], w, 0.0)
                w_ref[:, pl.ds(hh * t + j * LANES, LANES)] = w.astype(bf16)
                run = run + cs[:, LANES:]
            carry_ref[:, hl] = run

    def weighted_values(kb, w_ref):
        ks = key_rows(kb)
        for hh in range(heads_per_block):
            hl = pl.ds(hh * LANES, LANES)
            acc_ref[:, hl] += jnp.dot(w_ref[:, pl.ds(hh * t, t)], v_ref[ks, hl], preferred_element_type=f32)

    scores(qi, z0_ref)
    scores(jnp.maximum(qi - 1, 0), z1_ref)
    weights(z0_ref, w0_ref, masked=True)

    def two_steps(p, c):
        kb = qi - 1 - 2 * p
        weighted_values(kb + 1, w0_ref)
        weights(z1_ref, w1_ref)
        scores(jnp.maximum(kb - 1, 0), z0_ref)
        weighted_values(kb, w1_ref)
        weights(z0_ref, w0_ref)
        scores(jnp.maximum(kb - 2, 0), z1_ref)
        return c

    lax.fori_loop(0, qi // 2, two_steps, 0)

    @pl.when(qi % 2 == 1)
    def _():
        weighted_values(1, w0_ref)
        weights(z1_ref, w1_ref)
        weighted_values(0, w1_ref)

    @pl.when(qi % 2 == 0)
    def _():
        weighted_values(0, w0_ref)

    o_ref[...] = acc_ref[...].astype(o_ref.dtype)


def _sb_attention(qkv, *, batch, heads, head_dim, t=256, heads_per_block=4):
    n = qkv.shape[0]
    seq = n // batch
    t = min(t, seq)
    hpb = min(heads_per_block, heads)
    assert seq % t == 0 and t % LANES == 0 and head_dim == LANES and heads % hpb == 0
    nq = seq // t
    nhb = heads // hpb
    wb = hpb * head_dim
    tri = (jnp.arange(LANES)[:, None] > jnp.arange(LANES)[None, :]).astype(bf16)
    m = jnp.concatenate([tri, jnp.ones((LANES, LANES), bf16)], axis=1)
    m = jnp.concatenate([m, m], axis=0)
    return pl.pallas_call(
        functools.partial(_sb_kernel, heads_per_block=hpb),
        out_shape=jax.ShapeDtypeStruct((n, heads * head_dim), bf16),
        grid=(batch, nhb, nq),
        in_specs=[
            pl.BlockSpec((t, wb), lambda b, h, i: (b * nq + i, h)),
            pl.BlockSpec((seq, wb), lambda b, h, i: (b, nhb + h)),
            pl.BlockSpec((seq, wb), lambda b, h, i: (b, 2 * nhb + h)),
            pl.BlockSpec((2 * LANES, 2 * LANES), lambda b, h, i: (0, 0)),
        ],
        out_specs=pl.BlockSpec((t, wb), lambda b, h, i: (b * nq + i, h)),
        scratch_shapes=[pltpu.VMEM((t, wb), f32), pltpu.VMEM((t, wb), f32),
                        pltpu.VMEM((t, hpb * t), f32), pltpu.VMEM((t, hpb * t), f32),
                        pltpu.VMEM((t, hpb * t), bf16), pltpu.VMEM((t, hpb * t), bf16)],
        compiler_params=_params("parallel", "parallel", "arbitrary"),
        name="sb_attention",
    )(qkv, qkv, qkv, m)


def kernel(x, norm_ffn1, ffn1_w_gate, ffn1_w_up, ffn1_w_down, norm_mix, s5_w_in, s5_lam_re, s5_lam_im, s5_log_dt, s5_b_re, s5_b_im, s5_c_re, s5_c_im, s5_d, s5_w_glu, s5_b_glu, s5_w_out, sb_w_qkv, sb_g_q, sb_g_k, sb_w_o, norm_ffn2, ffn2_w_gate, ffn2_w_up, ffn2_w_down):
    batch, seq, d = x.shape
    depth = norm_ffn1.shape[0]
    head_dim = sb_g_q.shape[1]
    heads = sb_w_o.shape[1] // head_dim
    n_mixers = 2
    xf = x.reshape(batch * seq, d).astype(f32)
    for i in range(depth):
        xf = _ffn(xf, norm_ffn1[i], ffn1_w_gate[i].astype(bf16), ffn1_w_up[i].astype(bf16),
                  ffn1_w_down[i].astype(bf16))
        j = i // n_mixers
        if i % n_mixers == 0:
            u = _norm_matmul(xf, norm_mix[i], s5_w_in[j].astype(bf16), out_dtype=f32)
            y = _s5_scan(u, s5_lam_re[j], s5_lam_im[j], s5_log_dt[j], s5_b_re[j], s5_b_im[j],
                         s5_c_re[j], s5_c_im[j], s5_d[j], batch=batch)
            xf = _glu_out(y, xf, s5_w_glu[j].astype(bf16), s5_b_glu[j].astype(f32), s5_w_out[j].astype(bf16))
        else:
            hd = heads * head_dim
            q_scale = LOG2E / math.sqrt(head_dim)
            head_gain = jnp.concatenate([jnp.tile(sb_g_q[j].astype(f32) * q_scale, heads),
                                         jnp.tile(sb_g_k[j].astype(f32), heads)]).reshape(1, 2 * hd)
            qkv = _qkv(xf, norm_mix[i], sb_w_qkv[j].astype(bf16), head_gain, 2 * hd, head_dim)
            o = _sb_attention(qkv, batch=batch, heads=heads, head_dim=head_dim)
            xf = _matmul_res(o, sb_w_o[j].astype(bf16), xf)
        xf = _ffn(xf, norm_ffn2[i], ffn2_w_gate[i].astype(bf16), ffn2_w_up[i].astype(bf16),
                  ffn2_w_down[i].astype(bf16))
    return xf.reshape(batch, seq, d).astype(x.dtype)
```

```python
import functools
import math

import jax
import jax.numpy as jnp
from jax import lax
from jax.experimental import pallas as pl
from jax.experimental.pallas import tpu as pltpu

EPS = 1e-6
FFN_RES = 0.5
S5_LAM_RE_MAX = -1e-4
LOG2E = 1.4426950408889634
LANES = 128
SUBLANES = 8
VMEM_LIMIT = 56 * 1024 * 1024
S5_GROUPS_PER_BLOCK = 16

f32 = jnp.float32
bf16 = jnp.bfloat16


def _params(*sem):
    return pltpu.CompilerParams(dimension_semantics=sem, vmem_limit_bytes=VMEM_LIMIT)


def _rmsnorm(x, g):
    ms = jnp.mean(x * x, axis=-1, keepdims=True)
    return x * lax.rsqrt(ms + EPS) * g


def _ffn_kernel(x_ref, g_ref, wg_ref, wu_ref, wd_ref, o_ref, h_ref):
    @pl.when(pl.program_id(1) == 0)
    def _():
        x = x_ref[...]
        h_ref[...] = _rmsnorm(x, g_ref[...]).astype(bf16)
        o_ref[...] = x

    h = h_ref[...]
    gate = jnp.dot(h, wg_ref[...], preferred_element_type=f32)
    up = jnp.dot(h, wu_ref[...], preferred_element_type=f32)
    a = (gate * jax.nn.sigmoid(gate) * up * FFN_RES).astype(bf16)
    o_ref[...] += jnp.dot(a, wd_ref[...], preferred_element_type=f32)


def _ffn(x, g, wg, wu, wd, *, tm=1024, tf=512):
    n, d = x.shape
    f = wg.shape[1]
    tm, tf = min(tm, n), min(tf, f)
    assert n % tm == 0 and f % tf == 0
    return pl.pallas_call(
        _ffn_kernel,
        out_shape=jax.ShapeDtypeStruct((n, d), f32),
        grid=(n // tm, f // tf),
        in_specs=[
            pl.BlockSpec((tm, d), lambda i, j: (i, 0), pipeline_mode=pl.Buffered(1)),
            pl.BlockSpec((1, d), lambda i, j: (0, 0)),
            pl.BlockSpec((d, tf), lambda i, j: (0, j)),
            pl.BlockSpec((d, tf), lambda i, j: (0, j)),
            pl.BlockSpec((tf, d), lambda i, j: (j, 0)),
        ],
        out_specs=pl.BlockSpec((tm, d), lambda i, j: (i, 0)),
        scratch_shapes=[pltpu.VMEM((tm, d), bf16)],
        compiler_params=_params("parallel", "arbitrary"),
        name="ffn",
    )(x, g.reshape(1, d), wg, wu, wd)


def _norm_matmul_kernel(x_ref, g_ref, w_ref, o_ref, h_ref):
    @pl.when(pl.program_id(1) == 0)
    def _():
        h_ref[...] = _rmsnorm(x_ref[...], g_ref[...]).astype(bf16)

    o_ref[...] = jnp.dot(h_ref[...], w_ref[...], preferred_element_type=f32).astype(o_ref.dtype)


def _norm_matmul(x, g, w, *, out_dtype, tm=512, tn=2048):
    n, d = x.shape
    dout = w.shape[1]
    tm, tn = min(tm, n), min(tn, dout)
    assert n % tm == 0 and dout % tn == 0
    return pl.pallas_call(
        _norm_matmul_kernel,
        out_shape=jax.ShapeDtypeStruct((n, dout), out_dtype),
        grid=(n // tm, dout // tn),
        in_specs=[
            pl.BlockSpec((tm, d), lambda i, j: (i, 0)),
            pl.BlockSpec((1, d), lambda i, j: (0, 0)),
            pl.BlockSpec((d, tn), lambda i, j: (0, j)),
        ],
        out_specs=pl.BlockSpec((tm, tn), lambda i, j: (i, j)),
        scratch_shapes=[pltpu.VMEM((tm, d), bf16)],
        compiler_params=_params("parallel", "arbitrary"),
        name="norm_matmul",
    )(x, g.reshape(1, d), w)


def _qkv_kernel(x_ref, g_ref, w_ref, hg_ref, o_ref, h_ref, *, n_norm_tiles, head_dim):
    j = pl.program_id(1)

    @pl.when(j == 0)
    def _():
        h_ref[...] = _rmsnorm(x_ref[...], g_ref[...]).astype(bf16)

    acc = jnp.dot(h_ref[...], w_ref[...], preferred_element_type=f32)
    normed = j < n_norm_tiles
    hg = hg_ref[...]
    for c in range(0, acc.shape[1], head_dim):
        a = acc[:, c:c + head_dim]
        o_ref[:, c:c + head_dim] = jnp.where(normed, _rmsnorm(a, hg[:, c:c + head_dim]), a).astype(o_ref.dtype)


def _qkv(x, g, w, head_gain, n_norm_cols, head_dim, *, tm=1024, tn=512):
    n, d = x.shape
    dout = w.shape[1]
    tm, tn = min(tm, n), min(tn, dout)
    assert n % tm == 0 and dout % tn == 0 and n_norm_cols % tn == 0 and tn % head_dim == 0
    n_norm_tiles = n_norm_cols // tn
    return pl.pallas_call(
        functools.partial(_qkv_kernel, n_norm_tiles=n_norm_tiles, head_dim=head_dim),
        out_shape=jax.ShapeDtypeStruct((n, dout), bf16),
        grid=(n // tm, dout // tn),
        in_specs=[
            pl.BlockSpec((tm, d), lambda i, j: (i, 0)),
            pl.BlockSpec((1, d), lambda i, j: (0, 0)),
            pl.BlockSpec((d, tn), lambda i, j: (0, j)),
            pl.BlockSpec((1, tn), lambda i, j: (0, jnp.minimum(j, n_norm_tiles - 1))),
        ],
        out_specs=pl.BlockSpec((tm, tn), lambda i, j: (i, j)),
        scratch_shapes=[pltpu.VMEM((tm, d), bf16)],
        compiler_params=_params("parallel", "arbitrary"),
        name="qkv",
    )(x, g.reshape(1, d), w, head_gain)


def _matmul_res_kernel(a_ref, w_ref, x_ref, o_ref):
    o_ref[...] = x_ref[...] + jnp.dot(a_ref[...], w_ref[...], preferred_element_type=f32)


def _matmul_res(a, w, x, *, tm=512, tn=2048):
    n, k = a.shape
    d = w.shape[1]
    tm, tn = min(tm, n), min(tn, d)
    assert n % tm == 0 and d % tn == 0
    return pl.pallas_call(
        _matmul_res_kernel,
        out_shape=jax.ShapeDtypeStruct((n, d), f32),
        grid=(n // tm, d // tn),
        in_specs=[
            pl.BlockSpec((tm, k), lambda i, j: (i, 0)),
            pl.BlockSpec((k, tn), lambda i, j: (0, j)),
            pl.BlockSpec((tm, tn), lambda i, j: (i, j)),
        ],
        out_specs=pl.BlockSpec((tm, tn), lambda i, j: (i, j)),
        compiler_params=_params("parallel", "arbitrary"),
        name="matmul_res",
    )(a, w, x)


def _glu_out_kernel(yb_ref, yj_ref, x_ref, wg_ref, b_ref, wo_ref, o_ref):
    @pl.when(pl.program_id(1) == 0)
    def _():
        o_ref[...] = x_ref[...]

    z = jnp.dot(yb_ref[...], wg_ref[...], preferred_element_type=f32) + b_ref[...]
    a = (yj_ref[...] * jax.nn.sigmoid(z)).astype(bf16)
    o_ref[...] += jnp.dot(a, wo_ref[...], preferred_element_type=f32)


def _glu_out(yb, y, x, w_glu, b_glu, w_out, *, slabs, tm=1024, tn=512):
    n, c = y.shape
    d = w_out.shape[1]
    ns = n // slabs
    tm, tn = min(tm, ns), min(tn, c)
    assert n % slabs == 0 and ns % tm == 0 and c % tn == 0
    nit = ns // tm

    def token_block(i, j):
        return (i % nit, i // nit)

    out = pl.pallas_call(
        _glu_out_kernel,
        out_shape=jax.ShapeDtypeStruct((ns, slabs * d), f32),
        grid=(n // tm, c // tn),
        in_specs=[
            pl.BlockSpec((tm, c), lambda i, j: (i, 0)),
            pl.BlockSpec((tm, tn), lambda i, j: (i, j)),
            pl.BlockSpec((tm, d), token_block, pipeline_mode=pl.Buffered(1)),
            pl.BlockSpec((c, tn), lambda i, j: (0, j)),
            pl.BlockSpec((1, tn), lambda i, j: (0, j)),
            pl.BlockSpec((tn, d), lambda i, j: (j, 0)),
        ],
        out_specs=pl.BlockSpec((tm, d), token_block),
        compiler_params=_params("parallel", "arbitrary"),
        name="glu_out",
    )(yb, y, x.reshape(ns, slabs * d), w_glu, b_glu.reshape(1, c), w_out)
    return out.reshape(n, d)


S5_BLOCK = 8


def _s5_kernel(*refs, half):
    u_refs = refs[:S5_BLOCK]
    wb_ref, kt_ref, wc_ref, lam_ref, d_ref, o_ref, ob_ref, ub_ref, s_ref, cb_ref, carry_ref = refs[S5_BLOCK:]
    tr, cb = u_refs[0].shape

    @pl.when(pl.program_id(2) == 0)
    def _():
        carry_ref[...] = jnp.zeros_like(carry_ref)

    for r in range(S5_BLOCK):
        ub_ref[:, pl.ds(r * cb, cb)] = u_refs[r][...].astype(bf16)
    s_ref[...] = jnp.dot(ub_ref[...], wb_ref[0], preferred_element_type=f32)

    n_lane_tiles = half // LANES
    last = SUBLANES - 1
    first_row = lax.broadcasted_iota(jnp.int32, (SUBLANES, LANES), 0) == 0

    def row_tile(i, carry):
        rows = pl.ds(pl.multiple_of(i * SUBLANES, SUBLANES), SUBLANES)
        out = []
        for c in range(n_lane_tiles):
            re_l = pl.ds(c * LANES, LANES)
            im_l = pl.ds(half + c * LANES, LANES)
            cre, cim = carry[2 * c], carry[2 * c + 1]
            xre = s_ref[rows, re_l]
            xim = s_ref[rows, im_l]
            for k, dist in enumerate((1, 2, 4)):
                lre = lam_ref[0, 2 * k, :, re_l]
                lim = lam_ref[0, 2 * k + 1, :, re_l]
                pre = pltpu.roll(xre, dist, 0)
                pim = pltpu.roll(xim, dist, 0)
                xre, xim = xre + (lre * pre - lim * pim), xim + (lre * pim + lim * pre)
            are = lam_ref[0, 6, :, re_l]
            aim = lam_ref[0, 7, :, re_l]
            xre = xre + (are * cre - aim * cim)
            xim = xim + (are * cim + aim * cre)
            s_ref[rows, re_l] = jnp.where(first_row, cre, pltpu.roll(xre, 1, 0))
            s_ref[rows, im_l] = jnp.where(first_row, cim, pltpu.roll(xim, 1, 0))
            out.append(jnp.broadcast_to(xre[last:last + 1, :], xre.shape))
            out.append(jnp.broadcast_to(xim[last:last + 1, :], xim.shape))
        return tuple(out)

    carry0 = []
    for c in range(n_lane_tiles):
        carry0.append(carry_ref[:, pl.ds(c * LANES, LANES)])
        carry0.append(carry_ref[:, pl.ds(half + c * LANES, LANES)])
    carry = lax.fori_loop(0, tr // SUBLANES, row_tile, tuple(carry0))
    for c in range(n_lane_tiles):
        carry_ref[:, pl.ds(c * LANES, LANES)] = carry[2 * c]
        carry_ref[:, pl.ds(half + c * LANES, LANES)] = carry[2 * c + 1]

    cb_ref[...] = s_ref[...].astype(bf16)
    for j in range(S5_BLOCK):
        kj = (j + 1) * cb
        y = (jnp.dot(ub_ref[:, :kj], kt_ref[0, j, :kj, :], preferred_element_type=f32)
             + jnp.dot(cb_ref[...], wc_ref[0, j], preferred_element_type=f32)
             + d_ref[...] * u_refs[j][...])
        y = jax.nn.gelu(y)
        o_ref[j] = y
        ob_ref[j] = y.astype(bf16)


def _s5_discretise(lam_re, lam_im, log_dt, b_re, b_im):
    dt = jnp.exp(log_dt.astype(f32))[:, None]
    lr = jnp.minimum(lam_re.astype(f32), S5_LAM_RE_MAX)
    li = lam_im.astype(f32)
    mag = jnp.exp(lr * dt)
    ab_re = mag * jnp.cos(li * dt)
    ab_im = mag * jnp.sin(li * dt)
    den = lr * lr + li * li
    n_re = ab_re - 1.0
    f_re = (n_re * lr + ab_im * li) / den
    f_im = (ab_im * lr - n_re * li) / den
    br = b_re.astype(f32)
    bi = b_im.astype(f32)
    bb_re = f_re[..., None] * br - f_im[..., None] * bi
    bb_im = f_re[..., None] * bi + f_im[..., None] * br
    return lr * dt, li * dt, bb_re, bb_im


def _block_diag(w, gpb):
    g, a, b = w.shape
    w = w.reshape(g // gpb, gpb, a, b)
    eye = jnp.eye(gpb, dtype=w.dtype)
    return jnp.einsum("ngab,gh->ngahb", w, eye).reshape(g // gpb, gpb * a, gpb * b)


def _s5_scan_consts(z_re, z_im, gpb):
    g, p = z_re.shape

    def power(k):
        m = jnp.exp(z_re * k)
        return m * jnp.cos(z_im * k), m * jnp.sin(z_im * k)

    sub = jnp.arange(SUBLANES)[:, None, None]
    rows = []
    for dist in (1, 2, 4):
        pr, pi = power(float(dist))
        rows.append(jnp.where(sub >= dist, pr[None], 0.0))
        rows.append(jnp.where(sub >= dist, pi[None], 0.0))
    ks = (jnp.arange(SUBLANES, dtype=f32) + 1.0)[:, None, None]
    m = jnp.exp(z_re[None] * ks)
    rows.append(m * jnp.cos(z_im[None] * ks))
    rows.append(m * jnp.sin(z_im[None] * ks))
    c = jnp.stack(rows)
    c = c.reshape(8, SUBLANES, g // gpb, gpb * p)
    return c.transpose(2, 0, 1, 3)


def _s5_core(u, lam_re, lam_im, log_dt, b_re, b_im, c_re, c_im, d_skip, *, batch, tr=256):
    n, ch = u.shape
    g, p, h = b_re.shape
    nblk = n // S5_BLOCK
    blk_per_seq = nblk // batch
    gpb = min(S5_GROUPS_PER_BLOCK, g)
    tr = min(tr, blk_per_seq)
    assert n % (S5_BLOCK * batch) == 0 and g % gpb == 0 and blk_per_seq % tr == 0
    assert tr % SUBLANES == 0 and (gpb * p) % LANES == 0
    nb = g // gpb
    half = gpb * p
    cb = gpb * h
    nt = blk_per_seq // tr
    z_re, z_im, bb_re, bb_im = _s5_discretise(lam_re, lam_im, log_dt, b_re, b_im)
    cr = c_re.astype(f32)
    ci = c_im.astype(f32)

    def lam_pow(k):
        m = jnp.exp(z_re * k)
        return (m * jnp.cos(z_im * k))[:, :, None], (m * jnp.sin(z_im * k))[:, :, None]

    def lam_pow_b(k):
        pr, pi = lam_pow(float(k))
        return pr * bb_re - pi * bb_im, pr * bb_im + pi * bb_re

    wb = []
    for r in range(S5_BLOCK):
        wr, wi = lam_pow_b(S5_BLOCK - 1 - r)
        wb.append(jnp.concatenate([_block_diag(wr.transpose(0, 2, 1), gpb),
                                   _block_diag(wi.transpose(0, 2, 1), gpb)], axis=2))
    wb = jnp.concatenate(wb, axis=1).astype(bf16)
    kmat = []
    for tau in range(S5_BLOCK):
        wr, wi = lam_pow_b(tau)
        k = jnp.einsum("gop,gpi->gio", cr, wr) - jnp.einsum("gop,gpi->gio", ci, wi)
        kmat.append(_block_diag(k, gpb))
    zero = jnp.zeros_like(kmat[0])
    kt = jnp.stack([jnp.concatenate([kmat[j - r] if r <= j else zero for r in range(S5_BLOCK)], axis=1)
                    for j in range(S5_BLOCK)], axis=1).astype(bf16)
    wc = []
    for j in range(S5_BLOCK):
        pr, pi = lam_pow(float(j + 1))
        pr, pi = pr.transpose(0, 2, 1), pi.transpose(0, 2, 1)
        wc.append(jnp.concatenate([_block_diag((cr * pr - ci * pi).transpose(0, 2, 1), gpb),
                                   _block_diag(-(cr * pi + ci * pr).transpose(0, 2, 1), gpb)], axis=1))
    wc = jnp.stack(wc, axis=1).astype(bf16)
    lam = _s5_scan_consts(z_re * S5_BLOCK, z_im * S5_BLOCK, gpb)

    u8 = u.reshape(nblk, S5_BLOCK * ch)
    once = dict(pipeline_mode=pl.Buffered(1))
    u_specs = [pl.BlockSpec((tr, cb), lambda j, b, t, r=r: (b * nt + t, r * nb + j)) for r in range(S5_BLOCK)]
    out_spec = pl.BlockSpec((S5_BLOCK, tr, cb), lambda j, b, t: (0, b * nt + t, j))
    return pl.pallas_call(
        functools.partial(_s5_kernel, half=half),
        out_shape=(jax.ShapeDtypeStruct((S5_BLOCK, nblk, ch), f32),
                   jax.ShapeDtypeStruct((S5_BLOCK, nblk, ch), bf16)),
        grid=(nb, batch, nt),
        in_specs=u_specs + [
            pl.BlockSpec((1, S5_BLOCK * cb, 2 * half), lambda j, b, t: (j, 0, 0), **once),
            pl.BlockSpec((1, S5_BLOCK, S5_BLOCK * cb, cb), lambda j, b, t: (j, 0, 0, 0), **once),
            pl.BlockSpec((1, S5_BLOCK, 2 * half, cb), lambda j, b, t: (j, 0, 0, 0), **once),
            pl.BlockSpec((1, 8, SUBLANES, half), lambda j, b, t: (j, 0, 0, 0)),
            pl.BlockSpec((1, cb), lambda j, b, t: (0, j)),
        ],
        out_specs=(out_spec, out_spec),
        scratch_shapes=[pltpu.VMEM((tr, S5_BLOCK * cb), bf16), pltpu.VMEM((tr, 2 * half), f32),
                        pltpu.VMEM((tr, 2 * half), bf16), pltpu.VMEM((SUBLANES, 2 * half), f32)],
        compiler_params=_params("parallel", "parallel", "arbitrary"),
        name="s5_core",
    )(*([u8] * S5_BLOCK), wb, kt, wc, lam, d_skip.reshape(1, ch).astype(f32))


def _sb_kernel(q_ref, k_ref, v_ref, m_ref, o_ref, acc_ref, carry_ref, z0_ref, z1_ref, w0_ref, w1_ref,
               *, heads_per_block):
    t = q_ref.shape[0]
    nsub = t // LANES
    qi = pl.program_id(2)
    m = m_ref[...]
    row = lax.broadcasted_iota(jnp.int32, (t, t), 0)
    col = lax.broadcasted_iota(jnp.int32, (t, t), 1)
    past = col < row
    acc_ref[...] = jnp.zeros_like(acc_ref)
    carry_ref[...] = jnp.zeros_like(carry_ref)

    def key_rows(kb):
        return pl.ds(pl.multiple_of(kb * t, t), t)

    def scores(kb, z_ref):
        ks = key_rows(kb)
        for hh in range(heads_per_block):
            z_ref[:, pl.ds(hh * t, t)] = lax.dot_general(
                q_ref[:, pl.ds(hh * LANES, LANES)], k_ref[ks, pl.ds(hh * LANES, LANES)],
                (((1,), (1,)), ((), ())), preferred_element_type=f32)

    def weights(z_ref, w_ref, masked=False):
        for hh in range(heads_per_block):
            hl = pl.ds(hh * LANES, LANES)
            z = z_ref[:, pl.ds(hh * t, t)]
            zneg = jnp.minimum(z, 0.0)
            zdiff = zneg - z
            soft = jnp.log(1.0 + jnp.exp2(zdiff + zneg)) * LOG2E
            log_beta = zneg - soft
            log_keep = zdiff - soft
            if masked:
                log_keep = jnp.where(past, log_keep, 0.0)
            run = carry_ref[:, hl]
            for j in reversed(range(nsub)):
                sl = slice(j * LANES, (j + 1) * LANES)
                lk = log_keep[:, sl]
                hi = lk.astype(bf16)
                lo = (lk - hi.astype(f32)).astype(bf16)
                cs = jnp.dot(jnp.concatenate([hi, lo], axis=1), m, preferred_element_type=f32)
                w = jnp.exp2(log_beta[:, sl] + cs[:, :LANES] + run)
                if masked:
                    w = jnp.where(past[:, sl], w, 0.0)
                w_ref[:, pl.ds(hh * t + j * LANES, LANES)] = w.astype(bf16)
                run = run + cs[:, LANES:]
            carry_ref[:, hl] = run

    def weighted_values(kb, w_ref):
        ks = key_rows(kb)
        for hh in range(heads_per_block):
            hl = pl.ds(hh * LANES, LANES)
            acc_ref[:, hl] += jnp.dot(w_ref[:, pl.ds(hh * t, t)], v_ref[ks, hl], preferred_element_type=f32)

    scores(qi, z0_ref)
    scores(jnp.maximum(qi - 1, 0), z1_ref)
    weights(z0_ref, w0_ref, masked=True)

    def two_steps(p, c):
        kb = qi - 1 - 2 * p
        weighted_values(kb + 1, w0_ref)
        weights(z1_ref, w1_ref)
        scores(jnp.maximum(kb - 1, 0), z0_ref)
        weighted_values(kb, w1_ref)
        weights(z0_ref, w0_ref)
        scores(jnp.maximum(kb - 2, 0), z1_ref)
        return c

    lax.fori_loop(0, qi // 2, two_steps, 0)

    @pl.when(qi % 2 == 1)
    def _():
        weighted_values(1, w0_ref)
        weights(z1_ref, w1_ref)
        weighted_values(0, w1_ref)

    @pl.when(qi % 2 == 0)
    def _():
        weighted_values(0, w0_ref)

    o_ref[...] = acc_ref[...].astype(o_ref.dtype)


def _sb_attention(qkv, *, batch, heads, head_dim, t=256, heads_per_block=4):
    n = qkv.shape[0]
    seq = n // batch
    t = min(t, seq)
    hpb = min(heads_per_block, heads)
    assert seq % t == 0 and t % LANES == 0 and head_dim == LANES and heads % hpb == 0
    nq = seq // t
    nhb = heads // hpb
    wb = hpb * head_dim
    tri = (jnp.arange(LANES)[:, None] > jnp.arange(LANES)[None, :]).astype(bf16)
    m = jnp.concatenate([tri, jnp.ones((LANES, LANES), bf16)], axis=1)
    m = jnp.concatenate([m, m], axis=0)
    return pl.pallas_call(
        functools.partial(_sb_kernel, heads_per_block=hpb),
        out_shape=jax.ShapeDtypeStruct((n, heads * head_dim), bf16),
        grid=(batch, nhb, nq),
        in_specs=[
            pl.BlockSpec((t, wb), lambda b, h, i: (b * nq + i, h)),
            pl.BlockSpec((seq, wb), lambda b, h, i: (b, nhb + h)),
            pl.BlockSpec((seq, wb), lambda b, h, i: (b, 2 * nhb + h)),
            pl.BlockSpec((2 * LANES, 2 * LANES), lambda b, h, i: (0, 0)),
        ],
        out_specs=pl.BlockSpec((t, wb), lambda b, h, i: (b * nq + i, h)),
        scratch_shapes=[pltpu.VMEM((t, wb), f32), pltpu.VMEM((t, wb), f32),
                        pltpu.VMEM((t, hpb * t), f32), pltpu.VMEM((t, hpb * t), f32),
                        pltpu.VMEM((t, hpb * t), bf16), pltpu.VMEM((t, hpb * t), bf16)],
        compiler_params=_params("parallel", "parallel", "arbitrary"),
        name="sb_attention",
    )(qkv, qkv, qkv, m)


def kernel(x, norm_ffn1, ffn1_w_gate, ffn1_w_up, ffn1_w_down, norm_mix, s5_w_in, s5_lam_re, s5_lam_im, s5_log_dt, s5_b_re, s5_b_im, s5_c_re, s5_c_im, s5_d, s5_w_glu, s5_b_glu, s5_w_out, sb_w_qkv, sb_g_q, sb_g_k, sb_w_o, norm_ffn2, ffn2_w_gate, ffn2_w_up, ffn2_w_down):
    batch, seq, d = x.shape
    depth = norm_ffn1.shape[0]
    head_dim = sb_g_q.shape[1]
    heads = sb_w_o.shape[1] // head_dim
    n_mixers = 2
    xf = x.reshape(batch * seq, d).astype(f32)
    for i in range(depth):
        xf = _ffn(xf, norm_ffn1[i], ffn1_w_gate[i].astype(bf16), ffn1_w_up[i].astype(bf16),
                  ffn1_w_down[i].astype(bf16))
        j = i // n_mixers
        if i % n_mixers == 0:
            u = _norm_matmul(xf, norm_mix[i], s5_w_in[j].astype(bf16), out_dtype=f32)
            y, yb = _s5_core(u, s5_lam_re[j], s5_lam_im[j], s5_log_dt[j], s5_b_re[j], s5_b_im[j],
                             s5_c_re[j], s5_c_im[j], s5_d[j], batch=batch)
            n_tok, ch = u.shape
            xf = _glu_out(yb.reshape(n_tok, ch), y.reshape(n_tok, ch), xf, s5_w_glu[j].astype(bf16),
                          s5_b_glu[j].astype(f32), s5_w_out[j].astype(bf16), slabs=S5_BLOCK)
        else:
            hd = heads * head_dim
            q_scale = LOG2E / math.sqrt(head_dim)
            head_gain = jnp.concatenate([jnp.tile(sb_g_q[j].astype(f32) * q_scale, heads),
                                         jnp.tile(sb_g_k[j].astype(f32), heads)]).reshape(1, 2 * hd)
            qkv = _qkv(xf, norm_mix[i], sb_w_qkv[j].astype(bf16), head_gain, 2 * hd, head_dim)
            o = _sb_attention(qkv, batch=batch, heads=heads, head_dim=head_dim)
            xf = _matmul_res(o, sb_w_o[j].astype(bf16), xf)
        xf = _ffn(xf, norm_ffn2[i], ffn2_w_gate[i].astype(bf16), ffn2_w_up[i].astype(bf16),
                  ffn2_w_down[i].astype(bf16))
    return xf.reshape(batch, seq, d).astype(x.dtype)
```

```python
import functools
import math

import jax
import jax.numpy as jnp
from jax import lax
from jax.experimental import pallas as pl
from jax.experimental.pallas import tpu as pltpu

EPS = 1e-6
FFN_RES = 0.5
S5_LAM_RE_MAX = -1e-4
LOG2E = 1.4426950408889634
LANES = 128
SUBLANES = 8
VMEM_LIMIT = 56 * 1024 * 1024
S5_GROUPS_PER_BLOCK = 16

f32 = jnp.float32
bf16 = jnp.bfloat16


def _params(*sem):
    return pltpu.CompilerParams(dimension_semantics=sem, vmem_limit_bytes=VMEM_LIMIT)


def _rmsnorm(x, g):
    ms = jnp.mean(x * x, axis=-1, keepdims=True)
    return x * lax.rsqrt(ms + EPS) * g


CAST_BLOCK_BYTES = 6 * 1024 * 1024


def _cast_kernel(w_ref, o_ref):
    o_ref[...] = w_ref[...].astype(o_ref.dtype)


def _layer_bf16(w, layer):
    _, r, c = w.shape
    tr = r
    while tr % 2 == 0 and tr > SUBLANES and tr * c * w.dtype.itemsize > CAST_BLOCK_BYTES:
        tr //= 2
    return pl.pallas_call(
        _cast_kernel,
        out_shape=jax.ShapeDtypeStruct((r, c), bf16),
        grid=(r // tr,),
        in_specs=[pl.BlockSpec((None, tr, c), lambda i: (layer, i, 0))],
        out_specs=pl.BlockSpec((tr, c), lambda i: (i, 0)),
        compiler_params=_params("parallel"),
        name="layer_bf16",
    )(w)


def _ffn_kernel(x_ref, g_ref, wg_ref, wu_ref, wd_ref, o_ref, h_ref):
    @pl.when(pl.program_id(1) == 0)
    def _():
        x = x_ref[...]
        h_ref[...] = _rmsnorm(x, g_ref[...]).astype(bf16)
        o_ref[...] = x

    h = h_ref[...]
    gate = jnp.dot(h, wg_ref[...], preferred_element_type=f32)
    up = jnp.dot(h, wu_ref[...], preferred_element_type=f32)
    a = (gate * jax.nn.sigmoid(gate) * up * FFN_RES).astype(bf16)
    o_ref[...] += jnp.dot(a, wd_ref[...], preferred_element_type=f32)


def _ffn(x, g, wg, wu, wd, *, tm=512, tf=512):
    n, d = x.shape
    f = wg.shape[1]
    tm, tf = min(tm, n), min(tf, f)
    assert n % tm == 0 and f % tf == 0
    return pl.pallas_call(
        _ffn_kernel,
        out_shape=jax.ShapeDtypeStruct((n, d), f32),
        grid=(n // tm, f // tf),
        in_specs=[
            pl.BlockSpec((tm, d), lambda i, j: (i, 0)),
            pl.BlockSpec((1, d), lambda i, j: (0, 0)),
            pl.BlockSpec((d, tf), lambda i, j: (0, j)),
            pl.BlockSpec((d, tf), lambda i, j: (0, j)),
            pl.BlockSpec((tf, d), lambda i, j: (j, 0)),
        ],
        out_specs=pl.BlockSpec((tm, d), lambda i, j: (i, 0)),
        scratch_shapes=[pltpu.VMEM((tm, d), bf16)],
        compiler_params=_params("parallel", "arbitrary"),
        name="ffn",
    )(x, g.reshape(1, d), wg, wu, wd)


def _norm_matmul_kernel(x_ref, g_ref, w_ref, o_ref, h_ref):
    @pl.when(pl.program_id(1) == 0)
    def _():
        h_ref[...] = _rmsnorm(x_ref[...], g_ref[...]).astype(bf16)

    o_ref[...] = jnp.dot(h_ref[...], w_ref[...], preferred_element_type=f32).astype(o_ref.dtype)


def _norm_matmul(x, g, w, *, out_dtype, tm=512, tn=2048):
    n, d = x.shape
    dout = w.shape[1]
    tm, tn = min(tm, n), min(tn, dout)
    assert n % tm == 0 and dout % tn == 0
    return pl.pallas_call(
        _norm_matmul_kernel,
        out_shape=jax.ShapeDtypeStruct((n, dout), out_dtype),
        grid=(n // tm, dout // tn),
        in_specs=[
            pl.BlockSpec((tm, d), lambda i, j: (i, 0)),
            pl.BlockSpec((1, d), lambda i, j: (0, 0)),
            pl.BlockSpec((d, tn), lambda i, j: (0, j)),
        ],
        out_specs=pl.BlockSpec((tm, tn), lambda i, j: (i, j)),
        scratch_shapes=[pltpu.VMEM((tm, d), bf16)],
        compiler_params=_params("parallel", "arbitrary"),
        name="norm_matmul",
    )(x, g.reshape(1, d), w)


def _qkv_kernel(x_ref, g_ref, w_ref, hg_ref, o_ref, h_ref, *, n_norm_tiles, head_dim):
    j = pl.program_id(1)

    @pl.when(j == 0)
    def _():
        h_ref[...] = _rmsnorm(x_ref[...], g_ref[...]).astype(bf16)

    acc = jnp.dot(h_ref[...], w_ref[...], preferred_element_type=f32)
    normed = j < n_norm_tiles
    hg = hg_ref[...]
    for c in range(0, acc.shape[1], head_dim):
        a = acc[:, c:c + head_dim]
        o_ref[:, c:c + head_dim] = jnp.where(normed, _rmsnorm(a, hg[:, c:c + head_dim]), a).astype(o_ref.dtype)


def _qkv(x, g, w, head_gain, n_norm_cols, head_dim, *, tm=1024, tn=512):
    n, d = x.shape
    dout = w.shape[1]
    tm, tn = min(tm, n), min(tn, dout)
    assert n % tm == 0 and dout % tn == 0 and n_norm_cols % tn == 0 and tn % head_dim == 0
    n_norm_tiles = n_norm_cols // tn
    return pl.pallas_call(
        functools.partial(_qkv_kernel, n_norm_tiles=n_norm_tiles, head_dim=head_dim),
        out_shape=jax.ShapeDtypeStruct((n, dout), bf16),
        grid=(n // tm, dout // tn),
        in_specs=[
            pl.BlockSpec((tm, d), lambda i, j: (i, 0)),
            pl.BlockSpec((1, d), lambda i, j: (0, 0)),
            pl.BlockSpec((d, tn), lambda i, j: (0, j)),
            pl.BlockSpec((1, tn), lambda i, j: (0, jnp.minimum(j, n_norm_tiles - 1))),
        ],
        out_specs=pl.BlockSpec((tm, tn), lambda i, j: (i, j)),
        scratch_shapes=[pltpu.VMEM((tm, d), bf16)],
        compiler_params=_params("parallel", "arbitrary"),
        name="qkv",
    )(x, g.reshape(1, d), w, head_gain)


def _matmul_res_kernel(a_ref, w_ref, x_ref, o_ref):
    o_ref[...] = x_ref[...] + jnp.dot(a_ref[...], w_ref[...], preferred_element_type=f32)


def _matmul_res(a, w, x, *, tm=512, tn=2048):
    n, k = a.shape
    d = w.shape[1]
    tm, tn = min(tm, n), min(tn, d)
    assert n % tm == 0 and d % tn == 0
    return pl.pallas_call(
        _matmul_res_kernel,
        out_shape=jax.ShapeDtypeStruct((n, d), f32),
        grid=(n // tm, d // tn),
        in_specs=[
            pl.BlockSpec((tm, k), lambda i, j: (i, 0)),
            pl.BlockSpec((k, tn), lambda i, j: (0, j)),
            pl.BlockSpec((tm, tn), lambda i, j: (i, j)),
        ],
        out_specs=pl.BlockSpec((tm, tn), lambda i, j: (i, j)),
        compiler_params=_params("parallel", "arbitrary"),
        name="matmul_res",
    )(a, w, x)


def _glu_out_kernel(yb_ref, yj_ref, x_ref, wg_ref, b_ref, wo_ref, o_ref):
    @pl.when(pl.program_id(1) == 0)
    def _():
        o_ref[...] = x_ref[...]

    z = jnp.dot(yb_ref[...], wg_ref[...], preferred_element_type=f32) + b_ref[...]
    a = (yj_ref[...] * jax.nn.sigmoid(z)).astype(bf16)
    o_ref[...] += jnp.dot(a, wo_ref[...], preferred_element_type=f32)


def _glu_out(yb, y, x, w_glu, b_glu, w_out, *, slabs, tm=1024, tn=512):
    n, c = y.shape
    d = w_out.shape[1]
    ns = n // slabs
    tm, tn = min(tm, ns), min(tn, c)
    assert n % slabs == 0 and ns % tm == 0 and c % tn == 0
    nit = ns // tm

    def token_block(i, j):
        return (i % nit, i // nit)

    out = pl.pallas_call(
        _glu_out_kernel,
        out_shape=jax.ShapeDtypeStruct((ns, slabs * d), f32),
        grid=(n // tm, c // tn),
        in_specs=[
            pl.BlockSpec((tm, c), lambda i, j: (i, 0)),
            pl.BlockSpec((tm, tn), lambda i, j: (i, j)),
            pl.BlockSpec((tm, d), token_block),
            pl.BlockSpec((c, tn), lambda i, j: (0, j)),
            pl.BlockSpec((1, tn), lambda i, j: (0, j)),
            pl.BlockSpec((tn, d), lambda i, j: (j, 0)),
        ],
        out_specs=pl.BlockSpec((tm, d), token_block),
        compiler_params=_params("parallel", "arbitrary"),
        name="glu_out",
    )(yb, y, x.reshape(ns, slabs * d), w_glu, b_glu.reshape(1, c), w_out)
    return out.reshape(n, d)


S5_BLOCK = 8


def _s5_kernel(*refs, half):
    u_refs = refs[:S5_BLOCK]
    wb_ref, kt_ref, wc_ref, lam_ref, d_ref, o_ref, ob_ref, ub_ref, s_ref, cb_ref, carry_ref = refs[S5_BLOCK:]
    tr, cb = u_refs[0].shape

    @pl.when(pl.program_id(2) == 0)
    def _():
        carry_ref[...] = jnp.zeros_like(carry_ref)

    for r in range(S5_BLOCK):
        ub_ref[:, pl.ds((S5_BLOCK - 1 - r) * cb, cb)] = u_refs[r][...].astype(bf16)
    s_ref[...] = jnp.dot(ub_ref[...], wb_ref[0], preferred_element_type=f32)

    n_lane_tiles = half // LANES
    last = SUBLANES - 1
    first_row = lax.broadcasted_iota(jnp.int32, (SUBLANES, LANES), 0) == 0

    def row_tile(i, carry):
        rows = pl.ds(pl.multiple_of(i * SUBLANES, SUBLANES), SUBLANES)
        out = []
        for c in range(n_lane_tiles):
            re_l = pl.ds(c * LANES, LANES)
            im_l = pl.ds(half + c * LANES, LANES)
            cre, cim = carry[2 * c], carry[2 * c + 1]
            xre = s_ref[rows, re_l]
            xim = s_ref[rows, im_l]
            for k, dist in enumerate((1, 2, 4)):
                lre = lam_ref[0, 2 * k, :, re_l]
                lim = lam_ref[0, 2 * k + 1, :, re_l]
                pre = pltpu.roll(xre, dist, 0)
                pim = pltpu.roll(xim, dist, 0)
                xre, xim = xre + (lre * pre - lim * pim), xim + (lre * pim + lim * pre)
            are = lam_ref[0, 6, :, re_l]
            aim = lam_ref[0, 7, :, re_l]
            xre = xre + (are * cre - aim * cim)
            xim = xim + (are * cim + aim * cre)
            s_ref[rows, re_l] = jnp.where(first_row, cre, pltpu.roll(xre, 1, 0))
            s_ref[rows, im_l] = jnp.where(first_row, cim, pltpu.roll(xim, 1, 0))
            out.append(jnp.broadcast_to(xre[last:last + 1, :], xre.shape))
            out.append(jnp.broadcast_to(xim[last:last + 1, :], xim.shape))
        return tuple(out)

    carry0 = []
    for c in range(n_lane_tiles):
        carry0.append(carry_ref[:, pl.ds(c * LANES, LANES)])
        carry0.append(carry_ref[:, pl.ds(half + c * LANES, LANES)])
    carry = lax.fori_loop(0, tr // SUBLANES, row_tile, tuple(carry0))
    for c in range(n_lane_tiles):
        carry_ref[:, pl.ds(c * LANES, LANES)] = carry[2 * c]
        carry_ref[:, pl.ds(half + c * LANES, LANES)] = carry[2 * c + 1]

    cb_ref[...] = s_ref[...].astype(bf16)
    for j in range(S5_BLOCK):
        y = (jnp.dot(ub_ref[:, (S5_BLOCK - 1 - j) * cb:], kt_ref[0, :(j + 1) * cb, :], preferred_element_type=f32)
             + jnp.dot(cb_ref[...], wc_ref[0, j], preferred_element_type=f32)
             + d_ref[...] * u_refs[j][...])
        y = jax.nn.gelu(y)
        o_ref[j] = y
        ob_ref[j] = y.astype(bf16)


def _s5_discretise(lam_re, lam_im, log_dt, b_re, b_im):
    dt = jnp.exp(log_dt.astype(f32))[:, None]
    lr = jnp.minimum(lam_re.astype(f32), S5_LAM_RE_MAX)
    li = lam_im.astype(f32)
    mag = jnp.exp(lr * dt)
    ab_re = mag * jnp.cos(li * dt)
    ab_im = mag * jnp.sin(li * dt)
    den = lr * lr + li * li
    n_re = ab_re - 1.0
    f_re = (n_re * lr + ab_im * li) / den
    f_im = (ab_im * lr - n_re * li) / den
    br = b_re.astype(f32)
    bi = b_im.astype(f32)
    bb_re = f_re[..., None] * br - f_im[..., None] * bi
    bb_im = f_re[..., None] * bi + f_im[..., None] * br
    return lr * dt, li * dt, bb_re, bb_im


def _spread_groups(w, group_axis, new_axis):
    gpb = w.shape[group_axis]
    w = jnp.expand_dims(w, new_axis)
    old_axis = group_axis if group_axis < new_axis else group_axis + 1
    shape_old = [1] * w.ndim
    shape_old[old_axis] = gpb
    shape_new = [1] * w.ndim
    shape_new[new_axis] = gpb
    same = jnp.arange(gpb).reshape(shape_old) == jnp.arange(gpb).reshape(shape_new)
    return jnp.where(same, w, jnp.zeros((), w.dtype))


def _s5_scan_consts(z_re, z_im, gpb):
    g, p = z_re.shape

    def power(k):
        m = jnp.exp(z_re * k)
        return m * jnp.cos(z_im * k), m * jnp.sin(z_im * k)

    sub = jnp.arange(SUBLANES)[:, None, None]
    rows = []
    for dist in (1, 2, 4):
        pr, pi = power(float(dist))
        rows.append(jnp.where(sub >= dist, pr[None], 0.0))
        rows.append(jnp.where(sub >= dist, pi[None], 0.0))
    ks = (jnp.arange(SUBLANES, dtype=f32) + 1.0)[:, None, None]
    m = jnp.exp(z_re[None] * ks)
    rows.append(m * jnp.cos(z_im[None] * ks))
    rows.append(m * jnp.sin(z_im[None] * ks))
    c = jnp.stack(rows)
    c = c.reshape(8, SUBLANES, g // gpb, gpb * p)
    return c.transpose(2, 0, 1, 3)


def _s5_core(u, lam_re, lam_im, log_dt, b_re, b_im, c_re, c_im, d_skip, *, batch, tr=256):
    n, ch = u.shape
    g, p, h = b_re.shape
    nblk = n // S5_BLOCK
    blk_per_seq = nblk // batch
    gpb = min(S5_GROUPS_PER_BLOCK, g)
    tr = min(tr, blk_per_seq)
    assert n % (S5_BLOCK * batch) == 0 and g % gpb == 0 and blk_per_seq % tr == 0
    assert tr % SUBLANES == 0 and (gpb * p) % LANES == 0
    nb = g // gpb
    half = gpb * p
    cb = gpb * h
    nt = blk_per_seq // tr
    z_re, z_im, bb_re, bb_im = _s5_discretise(lam_re, lam_im, log_dt, b_re, b_im)
    cr = c_re.astype(f32)
    ci = c_im.astype(f32)

    ks = jnp.arange(S5_BLOCK + 1, dtype=f32)[:, None, None]
    mag = jnp.exp(z_re[None] * ks)
    pw_re = mag * jnp.cos(z_im[None] * ks)
    pw_im = mag * jnp.sin(z_im[None] * ks)
    lb_re = pw_re[:-1, :, :, None] * bb_re[None] - pw_im[:-1, :, :, None] * bb_im[None]
    lb_im = pw_re[:-1, :, :, None] * bb_im[None] + pw_im[:-1, :, :, None] * bb_re[None]
    wb = jnp.stack([lb_re, lb_im], axis=2)
    wb = wb.reshape(S5_BLOCK, nb, gpb, 2, p, h).transpose(1, 0, 2, 5, 3, 4)
    wb = _spread_groups(wb.astype(bf16), 2, 5).reshape(nb, S5_BLOCK * cb, 2 * half)
    kt = jnp.einsum("gop,tgpi->tgio", cr, lb_re) - jnp.einsum("gop,tgpi->tgio", ci, lb_im)
    kt = kt.reshape(S5_BLOCK, nb, gpb, h, h).transpose(1, 0, 2, 3, 4)
    kt = _spread_groups(kt.astype(bf16), 2, 4).reshape(nb, S5_BLOCK * cb, cb)
    cl_re = cr[None] * pw_re[1:, :, None, :] - ci[None] * pw_im[1:, :, None, :]
    cl_im = cr[None] * pw_im[1:, :, None, :] + ci[None] * pw_re[1:, :, None, :]
    wc = jnp.stack([cl_re, -cl_im], axis=2)
    wc = wc.reshape(S5_BLOCK, nb, gpb, 2, h, p).transpose(1, 0, 3, 2, 5, 4)
    wc = _spread_groups(wc.astype(bf16), 3, 5).reshape(nb, S5_BLOCK, 2 * half, cb)
    lam = _s5_scan_consts(z_re * S5_BLOCK, z_im * S5_BLOCK, gpb)

    u8 = u.reshape(nblk, S5_BLOCK * ch)
    once = dict(pipeline_mode=pl.Buffered(1))
    u_specs = [pl.BlockSpec((tr, cb), lambda j, b, t, r=r: (b * nt + t, r * nb + j)) for r in range(S5_BLOCK)]
    out_spec = pl.BlockSpec((S5_BLOCK, tr, cb), lambda j, b, t: (0, b * nt + t, j))
    return pl.pallas_call(
        functools.partial(_s5_kernel, half=half),
        out_shape=(jax.ShapeDtypeStruct((S5_BLOCK, nblk, ch), f32),
                   jax.ShapeDtypeStruct((S5_BLOCK, nblk, ch), bf16)),
        grid=(nb, batch, nt),
        in_specs=u_specs + [
            pl.BlockSpec((1, S5_BLOCK * cb, 2 * half), lambda j, b, t: (j, 0, 0), **once),
            pl.BlockSpec((1, S5_BLOCK * cb, cb), lambda j, b, t: (j, 0, 0), **once),
            pl.BlockSpec((1, S5_BLOCK, 2 * half, cb), lambda j, b, t: (j, 0, 0, 0), **once),
            pl.BlockSpec((1, 8, SUBLANES, half), lambda j, b, t: (j, 0, 0, 0)),
            pl.BlockSpec((1, cb), lambda j, b, t: (0, j)),
        ],
        out_specs=(out_spec, out_spec),
        scratch_shapes=[pltpu.VMEM((tr, S5_BLOCK * cb), bf16), pltpu.VMEM((tr, 2 * half), f32),
                        pltpu.VMEM((tr, 2 * half), bf16), pltpu.VMEM((SUBLANES, 2 * half), f32)],
        compiler_params=_params("parallel", "parallel", "arbitrary"),
        name="s5_core",
    )(*([u8] * S5_BLOCK), wb, kt, wc, lam, d_skip.reshape(1, ch).astype(f32))


def _sb_kernel(q_ref, k_ref, v_ref, m_ref, o_ref, acc_ref, carry_ref, z0_ref, z1_ref, w0_ref, w1_ref,
               *, heads_per_block):
    t = q_ref.shape[0]
    nsub = t // LANES
    qi = pl.program_id(2)
    m = m_ref[...]
    row = lax.broadcasted_iota(jnp.int32, (t, t), 0)
    col = lax.broadcasted_iota(jnp.int32, (t, t), 1)
    past = col < row
    acc_ref[...] = jnp.zeros_like(acc_ref)
    carry_ref[...] = jnp.zeros_like(carry_ref)

    def key_rows(kb):
        return pl.ds(pl.multiple_of(kb * t, t), t)

    all_heads = range(heads_per_block)

    def scores(kb, z_ref, heads=all_heads):
        ks = key_rows(kb)
        for hh in heads:
            z_ref[:, pl.ds(hh * t, t)] = lax.dot_general(
                q_ref[:, pl.ds(hh * LANES, LANES)], k_ref[ks, pl.ds(hh * LANES, LANES)],
                (((1,), (1,)), ((), ())), preferred_element_type=f32)

    def weights(z_ref, w_ref, heads=all_heads, masked=False):
        for hh in heads:
            hl = pl.ds(hh * LANES, LANES)
            z = z_ref[:, pl.ds(hh * t, t)]
            zneg = jnp.minimum(z, 0.0)
            zdiff = zneg - z
            soft = jnp.log(1.0 + jnp.exp2(zdiff + zneg)) * LOG2E
            log_beta = zneg - soft
            log_keep = zdiff - soft
            if masked:
                log_keep = jnp.where(past, log_keep, 0.0)
            run = carry_ref[:, hl]
            for j in reversed(range(nsub)):
                sl = slice(j * LANES, (j + 1) * LANES)
                lk = log_keep[:, sl]
                hi = lk.astype(bf16)
                lo = (lk - hi.astype(f32)).astype(bf16)
                cs = jnp.dot(jnp.concatenate([hi, lo], axis=1), m, preferred_element_type=f32)
                w = jnp.exp2(log_beta[:, sl] + cs[:, :LANES] + run)
                if masked:
                    w = jnp.where(past[:, sl], w, 0.0)
                w_ref[:, pl.ds(hh * t + j * LANES, LANES)] = w.astype(bf16)
                run = run + cs[:, LANES:]
            carry_ref[:, hl] = run

    def weighted_values(kb, w_ref, heads=all_heads):
        ks = key_rows(kb)
        for hh in heads:
            hl = pl.ds(hh * LANES, LANES)
            acc_ref[:, hl] += jnp.dot(w_ref[:, pl.ds(hh * t, t)], v_ref[ks, hl], preferred_element_type=f32)

    for hh in all_heads:
        scores(qi, z0_ref, (hh,))
        scores(jnp.maximum(qi - 1, 0), z1_ref, (hh,))
    for hh in all_heads:
        weights(z0_ref, w0_ref, (hh,), masked=True)

    def step(kb, z_ref, w_ref, z_next_ref, w_prev_ref):
        for hh in all_heads:
            weighted_values(kb + 1, w_prev_ref, (hh,))
            weights(z_ref, w_ref, (hh,))
            scores(jnp.maximum(kb - 1, 0), z_next_ref, (hh,))

    def two_steps(p, c):
        kb = qi - 1 - 2 * p
        step(kb, z1_ref, w1_ref, z0_ref, w0_ref)
        step(kb - 1, z0_ref, w0_ref, z1_ref, w1_ref)
        return c

    lax.fori_loop(0, qi // 2, two_steps, 0)

    @pl.when(qi % 2 == 1)
    def _():
        for hh in all_heads:
            weighted_values(1, w0_ref, (hh,))
            weights(z1_ref, w1_ref, (hh,))
        weighted_values(0, w1_ref)

    @pl.when(qi % 2 == 0)
    def _():
        weighted_values(0, w0_ref)

    o_ref[...] = acc_ref[...].astype(o_ref.dtype)


def _sb_attention(qkv, *, batch, heads, head_dim, t=256, heads_per_block=4):
    n = qkv.shape[0]
    seq = n // batch
    t = min(t, seq)
    hpb = min(heads_per_block, heads)
    assert seq % t == 0 and t % LANES == 0 and head_dim == LANES and heads % hpb == 0
    nq = seq // t
    nhb = heads // hpb
    wb = hpb * head_dim
    tri = (jnp.arange(LANES)[:, None] > jnp.arange(LANES)[None, :]).astype(bf16)
    m = jnp.concatenate([tri, jnp.ones((LANES, LANES), bf16)], axis=1)
    m = jnp.concatenate([m, m], axis=0)
    return pl.pallas_call(
        functools.partial(_sb_kernel, heads_per_block=hpb),
        out_shape=jax.ShapeDtypeStruct((n, heads * head_dim), bf16),
        grid=(batch, nhb, nq),
        in_specs=[
            pl.BlockSpec((t, wb), lambda b, h, i: (b * nq + i, h)),
            pl.BlockSpec((seq, wb), lambda b, h, i: (b, nhb + h)),
            pl.BlockSpec((seq, wb), lambda b, h, i: (b, 2 * nhb + h)),
            pl.BlockSpec((2 * LANES, 2 * LANES), lambda b, h, i: (0, 0)),
        ],
        out_specs=pl.BlockSpec((t, wb), lambda b, h, i: (b * nq + i, h)),
        scratch_shapes=[pltpu.VMEM((t, wb), f32), pltpu.VMEM((t, wb), f32),
                        pltpu.VMEM((t, hpb * t), f32), pltpu.VMEM((t, hpb * t), f32),
                        pltpu.VMEM((t, hpb * t), bf16), pltpu.VMEM((t, hpb * t), bf16)],
        compiler_params=_params("parallel", "parallel", "arbitrary"),
        name="sb_attention",
    )(qkv, qkv, qkv, m)


def kernel(x, norm_ffn1, ffn1_w_gate, ffn1_w_up, ffn1_w_down, norm_mix, s5_w_in, s5_lam_re, s5_lam_im, s5_log_dt, s5_b_re, s5_b_im, s5_c_re, s5_c_im, s5_d, s5_w_glu, s5_b_glu, s5_w_out, sb_w_qkv, sb_g_q, sb_g_k, sb_w_o, norm_ffn2, ffn2_w_gate, ffn2_w_up, ffn2_w_down):
    batch, seq, d = x.shape
    depth = norm_ffn1.shape[0]
    head_dim = sb_g_q.shape[1]
    heads = sb_w_o.shape[1] // head_dim
    n_mixers = 2
    xf = x.reshape(batch * seq, d).astype(f32)
    for i in range(depth):
        xf = _ffn(xf, norm_ffn1[i], _layer_bf16(ffn1_w_gate, i), _layer_bf16(ffn1_w_up, i),
                  _layer_bf16(ffn1_w_down, i))
        j = i // n_mixers
        if i % n_mixers == 0:
            u = _norm_matmul(xf, norm_mix[i], _layer_bf16(s5_w_in, j), out_dtype=f32)
            y, yb = _s5_core(u, s5_lam_re[j], s5_lam_im[j], s5_log_dt[j], s5_b_re[j], s5_b_im[j],
                             s5_c_re[j], s5_c_im[j], s5_d[j], batch=batch)
            n_tok, ch = u.shape
            xf = _glu_out(yb.reshape(n_tok, ch), y.reshape(n_tok, ch), xf, _layer_bf16(s5_w_glu, j),
                          s5_b_glu[j].astype(f32), _layer_bf16(s5_w_out, j), slabs=S5_BLOCK)
        else:
            hd = heads * head_dim
            q_scale = LOG2E / math.sqrt(head_dim)
            head_gain = jnp.concatenate([jnp.tile(sb_g_q[j].astype(f32) * q_scale, heads),
                                         jnp.tile(sb_g_k[j].astype(f32), heads)]).reshape(1, 2 * hd)
            qkv = _qkv(xf, norm_mix[i], _layer_bf16(sb_w_qkv, j), head_gain, 2 * hd, head_dim)
            o = _sb_attention(qkv, batch=batch, heads=heads, head_dim=head_dim)
            xf = _matmul_res(o, _layer_bf16(sb_w_o, j), xf)
        xf = _ffn(xf, norm_ffn2[i], _layer_bf16(ffn2_w_gate, i), _layer_bf16(ffn2_w_up, i),
                  _layer_bf16(ffn2_w_down, i))
    return xf.reshape(batch, seq, d).astype(x.dtype)
```

```python
import functools
import math

import jax
import jax.numpy as jnp
from jax import lax
from jax.experimental import pallas as pl
from jax.experimental.pallas import tpu as pltpu

EPS = 1e-6
FFN_RES = 0.5
S5_LAM_RE_MAX = -1e-4
LOG2E = 1.4426950408889634
LANES = 128
SUBLANES = 8
VMEM_LIMIT = 56 * 1024 * 1024
S5_GROUPS_PER_BLOCK = 16

f32 = jnp.float32
bf16 = jnp.bfloat16


def _params(*sem):
    return pltpu.CompilerParams(dimension_semantics=sem, vmem_limit_bytes=VMEM_LIMIT)


def _rmsnorm(x, g):
    ms = jnp.mean(x * x, axis=-1, keepdims=True)
    return x * lax.rsqrt(ms + EPS) * g


CAST_BLOCK_BYTES = 6 * 1024 * 1024


def _cast_kernel(w_ref, o_ref):
    o_ref[...] = w_ref[...].astype(o_ref.dtype)


def _layer_bf16(w, layer):
    _, r, c = w.shape
    tr = r
    while tr % 2 == 0 and tr > SUBLANES and tr * c * w.dtype.itemsize > CAST_BLOCK_BYTES:
        tr //= 2
    return pl.pallas_call(
        _cast_kernel,
        out_shape=jax.ShapeDtypeStruct((r, c), bf16),
        grid=(r // tr,),
        in_specs=[pl.BlockSpec((None, tr, c), lambda i: (layer, i, 0))],
        out_specs=pl.BlockSpec((tr, c), lambda i: (i, 0)),
        compiler_params=_params("parallel"),
        name="layer_bf16",
    )(w)


def _ffn_kernel(x_ref, g_ref, wg_ref, wu_ref, wd_ref, o_ref, h_ref):
    @pl.when(pl.program_id(1) == 0)
    def _():
        x = x_ref[...]
        h_ref[...] = _rmsnorm(x, g_ref[...]).astype(bf16)
        o_ref[...] = x

    h = h_ref[...]
    gate = jnp.dot(h, wg_ref[...], preferred_element_type=f32)
    up = jnp.dot(h, wu_ref[...], preferred_element_type=f32)
    a = (gate * jax.nn.sigmoid(gate) * up * FFN_RES).astype(bf16)
    o_ref[...] += jnp.dot(a, wd_ref[...], preferred_element_type=f32)


def _ffn(x, g, wg, wu, wd, *, tm=512, tf=512):
    n, d = x.shape
    f = wg.shape[1]
    tm, tf = min(tm, n), min(tf, f)
    assert n % tm == 0 and f % tf == 0
    return pl.pallas_call(
        _ffn_kernel,
        out_shape=jax.ShapeDtypeStruct((n, d), f32),
        grid=(n // tm, f // tf),
        in_specs=[
            pl.BlockSpec((tm, d), lambda i, j: (i, 0)),
            pl.BlockSpec((1, d), lambda i, j: (0, 0)),
            pl.BlockSpec((d, tf), lambda i, j: (0, j)),
            pl.BlockSpec((d, tf), lambda i, j: (0, j)),
            pl.BlockSpec((tf, d), lambda i, j: (j, 0)),
        ],
        out_specs=pl.BlockSpec((tm, d), lambda i, j: (i, 0)),
        scratch_shapes=[pltpu.VMEM((tm, d), bf16)],
        compiler_params=_params("parallel", "arbitrary"),
        name="ffn",
    )(x, g.reshape(1, d), wg, wu, wd)


def _norm_matmul_kernel(x_ref, g_ref, w_ref, o_ref, h_ref):
    @pl.when(pl.program_id(1) == 0)
    def _():
        h_ref[...] = _rmsnorm(x_ref[...], g_ref[...]).astype(bf16)

    o_ref[...] = jnp.dot(h_ref[...], w_ref[...], preferred_element_type=f32).astype(o_ref.dtype)


def _norm_matmul(x, g, w, *, out_dtype, tm=512, tn=2048):
    n, d = x.shape
    dout = w.shape[1]
    tm, tn = min(tm, n), min(tn, dout)
    assert n % tm == 0 and dout % tn == 0
    return pl.pallas_call(
        _norm_matmul_kernel,
        out_shape=jax.ShapeDtypeStruct((n, dout), out_dtype),
        grid=(n // tm, dout // tn),
        in_specs=[
            pl.BlockSpec((tm, d), lambda i, j: (i, 0)),
            pl.BlockSpec((1, d), lambda i, j: (0, 0)),
            pl.BlockSpec((d, tn), lambda i, j: (0, j)),
        ],
        out_specs=pl.BlockSpec((tm, tn), lambda i, j: (i, j)),
        scratch_shapes=[pltpu.VMEM((tm, d), bf16)],
        compiler_params=_params("parallel", "arbitrary"),
        name="norm_matmul",
    )(x, g.reshape(1, d), w)


def _qkv_kernel(x_ref, g_ref, w_ref, hg_ref, o_ref, h_ref, *, n_norm_tiles, head_dim):
    j = pl.program_id(1)

    @pl.when(j == 0)
    def _():
        h_ref[...] = _rmsnorm(x_ref[...], g_ref[...]).astype(bf16)

    acc = jnp.dot(h_ref[...], w_ref[...], preferred_element_type=f32)
    normed = j < n_norm_tiles
    hg = hg_ref[...]
    for c in range(0, acc.shape[1], head_dim):
        a = acc[:, c:c + head_dim]
        o_ref[:, c:c + head_dim] = jnp.where(normed, _rmsnorm(a, hg[:, c:c + head_dim]), a).astype(o_ref.dtype)


def _qkv(x, g, w, head_gain, n_norm_cols, head_dim, *, tm=1024, tn=512):
    n, d = x.shape
    dout = w.shape[1]
    tm, tn = min(tm, n), min(tn, dout)
    assert n % tm == 0 and dout % tn == 0 and n_norm_cols % tn == 0 and tn % head_dim == 0
    n_norm_tiles = n_norm_cols // tn
    return pl.pallas_call(
        functools.partial(_qkv_kernel, n_norm_tiles=n_norm_tiles, head_dim=head_dim),
        out_shape=jax.ShapeDtypeStruct((n, dout), bf16),
        grid=(n // tm, dout // tn),
        in_specs=[
            pl.BlockSpec((tm, d), lambda i, j: (i, 0)),
            pl.BlockSpec((1, d), lambda i, j: (0, 0)),
            pl.BlockSpec((d, tn), lambda i, j: (0, j)),
            pl.BlockSpec((1, tn), lambda i, j: (0, jnp.minimum(j, n_norm_tiles - 1))),
        ],
        out_specs=pl.BlockSpec((tm, tn), lambda i, j: (i, j)),
        scratch_shapes=[pltpu.VMEM((tm, d), bf16)],
        compiler_params=_params("parallel", "arbitrary"),
        name="qkv",
    )(x, g.reshape(1, d), w, head_gain)


def _matmul_res_kernel(a_ref, w_ref, x_ref, o_ref):
    o_ref[...] = x_ref[...] + jnp.dot(a_ref[...], w_ref[...], preferred_element_type=f32)


def _matmul_res(a, w, x, *, tm=512, tn=2048):
    n, k = a.shape
    d = w.shape[1]
    tm, tn = min(tm, n), min(tn, d)
    assert n % tm == 0 and d % tn == 0
    return pl.pallas_call(
        _matmul_res_kernel,
        out_shape=jax.ShapeDtypeStruct((n, d), f32),
        grid=(n // tm, d // tn),
        in_specs=[
            pl.BlockSpec((tm, k), lambda i, j: (i, 0)),
            pl.BlockSpec((k, tn), lambda i, j: (0, j)),
            pl.BlockSpec((tm, tn), lambda i, j: (i, j)),
        ],
        out_specs=pl.BlockSpec((tm, tn), lambda i, j: (i, j)),
        compiler_params=_params("parallel", "arbitrary"),
        name="matmul_res",
    )(a, w, x)


def _glu_out_kernel(yb_ref, yj_ref, x_ref, wg_ref, b_ref, wo_ref, o_ref, acc_ref):
    slabs, ti, c = yb_ref.shape
    j = pl.program_id(1)

    @pl.when(j == 0)
    def _():
        acc_ref[...] = jnp.zeros_like(acc_ref)

    z = jnp.dot(yb_ref[...].reshape(slabs * ti, c), wg_ref[...], preferred_element_type=f32) + b_ref[...]
    a = (yj_ref[...].reshape(slabs * ti, -1) * jax.nn.sigmoid(z)).astype(bf16)
    acc_ref[...] += jnp.dot(a, wo_ref[...], preferred_element_type=f32)

    @pl.when(j == pl.num_programs(1) - 1)
    def _():
        for s in range(slabs):
            o_ref[:, s, :] = x_ref[:, s, :] + acc_ref[pl.ds(s * ti, ti), :]


def _glu_out(yb, y, x, w_glu, b_glu, w_out, *, ti=128, tn=512):
    slabs, ns, c = y.shape
    n = slabs * ns
    d = w_out.shape[1]
    ti, tn = min(ti, ns), min(tn, c)
    assert ns % ti == 0 and c % tn == 0
    out = pl.pallas_call(
        _glu_out_kernel,
        out_shape=jax.ShapeDtypeStruct((ns, slabs, d), f32),
        grid=(ns // ti, c // tn),
        in_specs=[
            pl.BlockSpec((slabs, ti, c), lambda i, j: (0, i, 0)),
            pl.BlockSpec((slabs, ti, tn), lambda i, j: (0, i, j)),
            pl.BlockSpec((ti, slabs, d), lambda i, j: (i, 0, 0), pipeline_mode=pl.Buffered(1)),
            pl.BlockSpec((c, tn), lambda i, j: (0, j)),
            pl.BlockSpec((1, tn), lambda i, j: (0, j)),
            pl.BlockSpec((tn, d), lambda i, j: (j, 0)),
        ],
        out_specs=pl.BlockSpec((ti, slabs, d), lambda i, j: (i, 0, 0)),
        scratch_shapes=[pltpu.VMEM((slabs * ti, d), f32)],
        compiler_params=_params("parallel", "arbitrary"),
        name="glu_out",
    )(yb, y, x.reshape(ns, slabs, d), w_glu, b_glu.reshape(1, c), w_out)
    return out.reshape(n, d)


S5_BLOCK = 8


def _s5_kernel(*refs, half, n_u):
    u_refs = refs[:n_u]
    wb_ref, kt_ref, wc_ref, lam_ref, d_ref, o_ref, ob_ref, uf_ref, ub_ref, s_ref, cb_ref, carry_ref = refs[n_u:]
    _, tr, cb = uf_ref.shape

    @pl.when(pl.program_id(2) == 0)
    def _():
        carry_ref[...] = jnp.zeros_like(carry_ref)

    for r in range(S5_BLOCK):
        for k in range(n_u):
            ur = u_refs[k][pl.ds(r, tr, stride=S5_BLOCK), :]
            uf_ref[r, :, pl.ds(k * LANES, LANES)] = ur
            ub_ref[:, pl.ds((S5_BLOCK - 1 - r) * cb + k * LANES, LANES)] = ur.astype(bf16)
    s_ref[...] = jnp.dot(ub_ref[...], wb_ref[0], preferred_element_type=f32)

    n_lane_tiles = half // LANES
    last = SUBLANES - 1
    first_row = lax.broadcasted_iota(jnp.int32, (SUBLANES, LANES), 0) == 0

    def row_tile(i, carry):
        rows = pl.ds(pl.multiple_of(i * SUBLANES, SUBLANES), SUBLANES)
        out = []
        for c in range(n_lane_tiles):
            re_l = pl.ds(c * LANES, LANES)
            im_l = pl.ds(half + c * LANES, LANES)
            cre, cim = carry[2 * c], carry[2 * c + 1]
            xre = s_ref[rows, re_l]
            xim = s_ref[rows, im_l]
            for k, dist in enumerate((1, 2, 4)):
                lre = lam_ref[0, 2 * k, :, re_l]
                lim = lam_ref[0, 2 * k + 1, :, re_l]
                pre = pltpu.roll(xre, dist, 0)
                pim = pltpu.roll(xim, dist, 0)
                xre, xim = xre + (lre * pre - lim * pim), xim + (lre * pim + lim * pre)
            are = lam_ref[0, 6, :, re_l]
            aim = lam_ref[0, 7, :, re_l]
            xre = xre + (are * cre - aim * cim)
            xim = xim + (are * cim + aim * cre)
            s_ref[rows, re_l] = jnp.where(first_row, cre, pltpu.roll(xre, 1, 0))
            s_ref[rows, im_l] = jnp.where(first_row, cim, pltpu.roll(xim, 1, 0))
            out.append(jnp.broadcast_to(xre[last:last + 1, :], xre.shape))
            out.append(jnp.broadcast_to(xim[last:last + 1, :], xim.shape))
        return tuple(out)

    carry0 = []
    for c in range(n_lane_tiles):
        carry0.append(carry_ref[:, pl.ds(c * LANES, LANES)])
        carry0.append(carry_ref[:, pl.ds(half + c * LANES, LANES)])
    carry = lax.fori_loop(0, tr // SUBLANES, row_tile, tuple(carry0))
    for c in range(n_lane_tiles):
        carry_ref[:, pl.ds(c * LANES, LANES)] = carry[2 * c]
        carry_ref[:, pl.ds(half + c * LANES, LANES)] = carry[2 * c + 1]

    cb_ref[...] = s_ref[...].astype(bf16)
    for j in range(S5_BLOCK):
        y = (jnp.dot(ub_ref[:, (S5_BLOCK - 1 - j) * cb:], kt_ref[0, :(j + 1) * cb, :], preferred_element_type=f32)
             + jnp.dot(cb_ref[...], wc_ref[0, j], preferred_element_type=f32)
             + d_ref[...] * uf_ref[j])
        y = jax.nn.gelu(y)
        o_ref[j] = y
        ob_ref[j] = y.astype(bf16)


def _s5_discretise(lam_re, lam_im, log_dt, b_re, b_im):
    dt = jnp.exp(log_dt.astype(f32))[:, None]
    lr = jnp.minimum(lam_re.astype(f32), S5_LAM_RE_MAX)
    li = lam_im.astype(f32)
    mag = jnp.exp(lr * dt)
    ab_re = mag * jnp.cos(li * dt)
    ab_im = mag * jnp.sin(li * dt)
    den = lr * lr + li * li
    n_re = ab_re - 1.0
    f_re = (n_re * lr + ab_im * li) / den
    f_im = (ab_im * lr - n_re * li) / den
    br = b_re.astype(f32)
    bi = b_im.astype(f32)
    bb_re = f_re[..., None] * br - f_im[..., None] * bi
    bb_im = f_re[..., None] * bi + f_im[..., None] * br
    return lr * dt, li * dt, bb_re, bb_im


def _expand_kernel(v_ref, e_ref, o_ref, *, row_group, col_group, groups, lhs_transposed):
    v = v_ref[0]
    if lhs_transposed:
        t = lax.dot_general(v, e_ref[...], (((0,), (0,)), ((), ())), preferred_element_type=f32)
    else:
        t = jnp.dot(v, e_ref[...], preferred_element_type=f32)
    rows = lax.broadcasted_iota(jnp.int32, t.shape, 0)
    cols = lax.broadcasted_iota(jnp.int32, t.shape, 1)
    row_g = lax.shift_right_logical(rows, row_group.bit_length() - 1) & (groups - 1)
    col_g = lax.shift_right_logical(cols, col_group.bit_length() - 1) & (groups - 1)
    o_ref[0] = jnp.where(row_g == col_g, t, 0.0).astype(o_ref.dtype)


def _expand_block_diag(v, e, *, row_group, col_group, groups, lhs_transposed=False):
    nlead = v.shape[0]
    rows = v.shape[2] if lhs_transposed else v.shape[1]
    cols = e.shape[1]
    for size in (row_group, col_group, groups):
        assert size & (size - 1) == 0
    return pl.pallas_call(
        functools.partial(_expand_kernel, row_group=row_group, col_group=col_group, groups=groups,
                          lhs_transposed=lhs_transposed),
        out_shape=jax.ShapeDtypeStruct((nlead, rows, cols), bf16),
        grid=(nlead,),
        in_specs=[pl.BlockSpec((1,) + v.shape[1:], lambda i: (i, 0, 0)),
                  pl.BlockSpec(e.shape, lambda i: (0, 0))],
        out_specs=pl.BlockSpec((1, rows, cols), lambda i: (i, 0, 0)),
        compiler_params=_params("parallel"),
        name="expand_block_diag",
    )(v, e)


def _s5_scan_consts(z_re, z_im, gpb):
    g, p = z_re.shape

    def power(k):
        m = jnp.exp(z_re * k)
        return m * jnp.cos(z_im * k), m * jnp.sin(z_im * k)

    sub = jnp.arange(SUBLANES)[:, None, None]
    rows = []
    for dist in (1, 2, 4):
        pr, pi = power(float(dist))
        rows.append(jnp.where(sub >= dist, pr[None], 0.0))
        rows.append(jnp.where(sub >= dist, pi[None], 0.0))
    ks = (jnp.arange(SUBLANES, dtype=f32) + 1.0)[:, None, None]
    m = jnp.exp(z_re[None] * ks)
    rows.append(m * jnp.cos(z_im[None] * ks))
    rows.append(m * jnp.sin(z_im[None] * ks))
    c = jnp.stack(rows)
    c = c.reshape(8, SUBLANES, g // gpb, gpb * p)
    return c.transpose(2, 0, 1, 3)


def _s5_core(u, lam_re, lam_im, log_dt, b_re, b_im, c_re, c_im, d_skip, *, batch, tr=256):
    n, ch = u.shape
    g, p, h = b_re.shape
    nblk = n // S5_BLOCK
    blk_per_seq = nblk // batch
    gpb = min(S5_GROUPS_PER_BLOCK, g)
    tr = min(tr, blk_per_seq)
    assert n % (S5_BLOCK * batch) == 0 and g % gpb == 0 and blk_per_seq % tr == 0
    assert tr % SUBLANES == 0 and (gpb * p) % LANES == 0 and (gpb * h) % LANES == 0
    nb = g // gpb
    half = gpb * p
    cb = gpb * h
    nt = blk_per_seq // tr
    z_re, z_im, bb_re, bb_im = _s5_discretise(lam_re, lam_im, log_dt, b_re, b_im)
    cr = c_re.astype(f32)
    ci = c_im.astype(f32)

    ks = jnp.arange(S5_BLOCK + 1, dtype=f32)[:, None, None]
    mag = jnp.exp(z_re[None] * ks)
    pw_re = mag * jnp.cos(z_im[None] * ks)
    pw_im = mag * jnp.sin(z_im[None] * ks)
    lb_re = pw_re[:-1, :, :, None] * bb_re[None] - pw_im[:-1, :, :, None] * bb_im[None]
    lb_im = pw_re[:-1, :, :, None] * bb_im[None] + pw_im[:-1, :, :, None] * bb_re[None]
    lane = jnp.arange(2 * half)
    col = jnp.arange(2 * p)
    rep_state = ((lane[None, :] // half == col[:, None] // p) & (lane[None, :] % p == col[:, None] % p)).astype(bf16)
    rep_chan = (jnp.arange(cb)[None, :] % h == jnp.arange(h)[:, None]).astype(bf16)
    wb = jnp.stack([lb_re, lb_im], axis=2)
    wb = wb.reshape(S5_BLOCK, nb, gpb, 2, p, h).transpose(1, 0, 2, 5, 3, 4)
    wb = _expand_block_diag(wb.reshape(nb * S5_BLOCK, cb, 2 * p).astype(bf16), rep_state,
                            row_group=h, col_group=p, groups=gpb).reshape(nb, S5_BLOCK * cb, 2 * half)
    kt = jnp.einsum("gop,tgpi->tgio", cr, lb_re) - jnp.einsum("gop,tgpi->tgio", ci, lb_im)
    kt = kt.reshape(S5_BLOCK, nb, gpb, h, h).transpose(1, 0, 2, 3, 4)
    kt = _expand_block_diag(kt.reshape(nb * S5_BLOCK, cb, h).astype(bf16), rep_chan,
                            row_group=h, col_group=h, groups=gpb).reshape(nb, S5_BLOCK * cb, cb)
    cl_re = cr[None] * pw_re[1:, :, None, :] - ci[None] * pw_im[1:, :, None, :]
    cl_im = cr[None] * pw_im[1:, :, None, :] + ci[None] * pw_re[1:, :, None, :]
    wc = jnp.stack([cl_re, -cl_im], axis=2)
    wc = wc.reshape(S5_BLOCK, nb, gpb, 2, h, p).transpose(1, 0, 4, 3, 2, 5)
    wc = _expand_block_diag(wc.reshape(nb * S5_BLOCK, h, 2 * half).astype(bf16), rep_chan, row_group=p,
                            col_group=h, groups=gpb, lhs_transposed=True).reshape(nb, S5_BLOCK, 2 * half, cb)
    lam = _s5_scan_consts(z_re * S5_BLOCK, z_im * S5_BLOCK, gpb)

    once = dict(pipeline_mode=pl.Buffered(1))
    n_u = cb // LANES
    u_specs = [pl.BlockSpec((S5_BLOCK * tr, LANES), lambda j, b, t, k=k: (b * nt + t, j * n_u + k))
               for k in range(n_u)]
    out_spec = pl.BlockSpec((S5_BLOCK, tr, cb), lambda j, b, t: (0, b * nt + t, j))
    return pl.pallas_call(
        functools.partial(_s5_kernel, half=half, n_u=n_u),
        out_shape=(jax.ShapeDtypeStruct((S5_BLOCK, nblk, ch), f32),
                   jax.ShapeDtypeStruct((S5_BLOCK, nblk, ch), bf16)),
        grid=(nb, batch, nt),
        in_specs=u_specs + [
            pl.BlockSpec((1, S5_BLOCK * cb, 2 * half), lambda j, b, t: (j, 0, 0), **once),
            pl.BlockSpec((1, S5_BLOCK * cb, cb), lambda j, b, t: (j, 0, 0), **once),
            pl.BlockSpec((1, S5_BLOCK, 2 * half, cb), lambda j, b, t: (j, 0, 0, 0), **once),
            pl.BlockSpec((1, 8, SUBLANES, half), lambda j, b, t: (j, 0, 0, 0)),
            pl.BlockSpec((1, cb), lambda j, b, t: (0, j)),
        ],
        out_specs=(out_spec, out_spec),
        scratch_shapes=[pltpu.VMEM((S5_BLOCK, tr, cb), f32),
                        pltpu.VMEM((tr, S5_BLOCK * cb), bf16), pltpu.VMEM((tr, 2 * half), f32),
                        pltpu.VMEM((tr, 2 * half), bf16), pltpu.VMEM((SUBLANES, 2 * half), f32)],
        compiler_params=_params("parallel", "parallel", "arbitrary"),
        name="s5_core",
    )(*([u] * n_u), wb, kt, wc, lam, d_skip.reshape(1, ch).astype(f32))


def _sb_kernel(q_ref, k_ref, v_ref, m_ref, o_ref, acc_ref, carry_ref, z0_ref, z1_ref, w0_ref, w1_ref,
               *, heads_per_block):
    t = q_ref.shape[0]
    nsub = t // LANES
    qi = pl.program_id(2)
    m = m_ref[...]
    row = lax.broadcasted_iota(jnp.int32, (t, t), 0)
    col = lax.broadcasted_iota(jnp.int32, (t, t), 1)
    past = col < row
    acc_ref[...] = jnp.zeros_like(acc_ref)
    carry_ref[...] = jnp.zeros_like(carry_ref)

    def key_rows(kb):
        return pl.ds(pl.multiple_of(kb * t, t), t)

    all_heads = range(heads_per_block)

    def scores(kb, z_ref, heads=all_heads):
        ks = key_rows(kb)
        for hh in heads:
            z_ref[:, pl.ds(hh * t, t)] = lax.dot_general(
                q_ref[:, pl.ds(hh * LANES, LANES)], k_ref[ks, pl.ds(hh * LANES, LANES)],
                (((1,), (1,)), ((), ())), preferred_element_type=f32)

    def weights(z_ref, w_ref, heads=all_heads, masked=False):
        for hh in heads:
            hl = pl.ds(hh * LANES, LANES)
            z = z_ref[:, pl.ds(hh * t, t)]
            zneg = jnp.minimum(z, 0.0)
            zdiff = zneg - z
            soft = jnp.log(1.0 + jnp.exp2(zdiff + zneg)) * LOG2E
            log_beta = zneg - soft
            log_keep = zdiff - soft
            if masked:
                log_keep = jnp.where(past, log_keep, 0.0)
            run = carry_ref[:, hl]
            for j in reversed(range(nsub)):
                sl = slice(j * LANES, (j + 1) * LANES)
                lk = log_keep[:, sl]
                hi = lk.astype(bf16)
                lo = (lk - hi.astype(f32)).astype(bf16)
                cs = jnp.dot(jnp.concatenate([hi, lo], axis=1), m, preferred_element_type=f32)
                w = jnp.exp2(log_beta[:, sl] + cs[:, :LANES] + run)
                if masked:
                    w = jnp.where(past[:, sl], w, 0.0)
                w_ref[:, pl.ds(hh * t + j * LANES, LANES)] = w.astype(bf16)
                run = run + cs[:, LANES:]
            carry_ref[:, hl] = run

    def weighted_values(kb, w_ref, heads=all_heads):
        ks = key_rows(kb)
        for hh in heads:
            hl = pl.ds(hh * LANES, LANES)
            acc_ref[:, hl] += jnp.dot(w_ref[:, pl.ds(hh * t, t)], v_ref[ks, hl], preferred_element_type=f32)

    for hh in all_heads:
        scores(qi, z0_ref, (hh,))
        scores(jnp.maximum(qi - 1, 0), z1_ref, (hh,))
    for hh in all_heads:
        weights(z0_ref, w0_ref, (hh,), masked=True)

    def step(kb, z_ref, w_ref, z_next_ref, w_prev_ref):
        for hh in all_heads:
            weighted_values(kb + 1, w_prev_ref, (hh,))
            weights(z_ref, w_ref, (hh,))
            scores(jnp.maximum(kb - 1, 0), z_next_ref, (hh,))

    def two_steps(p, c):
        kb = qi - 1 - 2 * p
        step(kb, z1_ref, w1_ref, z0_ref, w0_ref)
        step(kb - 1, z0_ref, w0_ref, z1_ref, w1_ref)
        return c

    lax.fori_loop(0, qi // 2, two_steps, 0)

    @pl.when(qi % 2 == 1)
    def _():
        for hh in all_heads:
            weighted_values(1, w0_ref, (hh,))
            weights(z1_ref, w1_ref, (hh,))
        weighted_values(0, w1_ref)

    @pl.when(qi % 2 == 0)
    def _():
        weighted_values(0, w0_ref)

    o_ref[...] = acc_ref[...].astype(o_ref.dtype)


def _sb_attention(qkv, *, batch, heads, head_dim, t=256, heads_per_block=4):
    n = qkv.shape[0]
    seq = n // batch
    t = min(t, seq)
    hpb = min(heads_per_block, heads)
    assert seq % t == 0 and t % LANES == 0 and head_dim == LANES and heads % hpb == 0
    nq = seq // t
    nhb = heads // hpb
    wb = hpb * head_dim
    tri = (jnp.arange(LANES)[:, None] > jnp.arange(LANES)[None, :]).astype(bf16)
    m = jnp.concatenate([tri, jnp.ones((LANES, LANES), bf16)], axis=1)
    m = jnp.concatenate([m, m], axis=0)
    return pl.pallas_call(
        functools.partial(_sb_kernel, heads_per_block=hpb),
        out_shape=jax.ShapeDtypeStruct((n, heads * head_dim), bf16),
        grid=(batch, nhb, nq),
        in_specs=[
            pl.BlockSpec((t, wb), lambda b, h, i: (b * nq + i, h)),
            pl.BlockSpec((seq, wb), lambda b, h, i: (b, nhb + h)),
            pl.BlockSpec((seq, wb), lambda b, h, i: (b, 2 * nhb + h)),
            pl.BlockSpec((2 * LANES, 2 * LANES), lambda b, h, i: (0, 0)),
        ],
        out_specs=pl.BlockSpec((t, wb), lambda b, h, i: (b * nq + i, h)),
        scratch_shapes=[pltpu.VMEM((t, wb), f32), pltpu.VMEM((t, wb), f32),
                        pltpu.VMEM((t, hpb * t), f32), pltpu.VMEM((t, hpb * t), f32),
                        pltpu.VMEM((t, hpb * t), bf16), pltpu.VMEM((t, hpb * t), bf16)],
        compiler_params=_params("parallel", "parallel", "arbitrary"),
        name="sb_attention",
    )(qkv, qkv, qkv, m)


def kernel(x, norm_ffn1, ffn1_w_gate, ffn1_w_up, ffn1_w_down, norm_mix, s5_w_in, s5_lam_re, s5_lam_im, s5_log_dt, s5_b_re, s5_b_im, s5_c_re, s5_c_im, s5_d, s5_w_glu, s5_b_glu, s5_w_out, sb_w_qkv, sb_g_q, sb_g_k, sb_w_o, norm_ffn2, ffn2_w_gate, ffn2_w_up, ffn2_w_down):
    batch, seq, d = x.shape
    depth = norm_ffn1.shape[0]
    head_dim = sb_g_q.shape[1]
    heads = sb_w_o.shape[1] // head_dim
    n_mixers = 2
    xf = x.reshape(batch * seq, d).astype(f32)
    for i in range(depth):
        xf = _ffn(xf, norm_ffn1[i], _layer_bf16(ffn1_w_gate, i), _layer_bf16(ffn1_w_up, i),
                  _layer_bf16(ffn1_w_down, i))
        j = i // n_mixers
        if i % n_mixers == 0:
            u = _norm_matmul(xf, norm_mix[i], _layer_bf16(s5_w_in, j), out_dtype=f32)
            y, yb = _s5_core(u, s5_lam_re[j], s5_lam_im[j], s5_log_dt[j], s5_b_re[j], s5_b_im[j],
                             s5_c_re[j], s5_c_im[j], s5_d[j], batch=batch)
            xf = _glu_out(yb, y, xf, _layer_bf16(s5_w_glu, j), s5_b_glu[j].astype(f32), _layer_bf16(s5_w_out, j))
        else:
            hd = heads * head_dim
            q_scale = LOG2E / math.sqrt(head_dim)
            head_gain = jnp.concatenate([jnp.tile(sb_g_q[j].astype(f32) * q_scale, heads),
                                         jnp.tile(sb_g_k[j].astype(f32), heads)]).reshape(1, 2 * hd)
            qkv = _qkv(xf, norm_mix[i], _layer_bf16(sb_w_qkv, j), head_gain, 2 * hd, head_dim)
            o = _sb_attention(qkv, batch=batch, heads=heads, head_dim=head_dim)
            xf = _matmul_res(o, _layer_bf16(sb_w_o, j), xf)
        xf = _ffn(xf, norm_ffn2[i], _layer_bf16(ffn2_w_gate, i), _layer_bf16(ffn2_w_up, i),
                  _layer_bf16(ffn2_w_down, i))
    return xf.reshape(batch, seq, d).astype(x.dtype)
```

```python
import functools
import math

import jax
import jax.numpy as jnp
from jax import lax
from jax.experimental import pallas as pl
from jax.experimental.pallas import tpu as pltpu

EPS = 1e-6
FFN_RES = 0.5
S5_LAM_RE_MAX = -1e-4
LOG2E = 1.4426950408889634
LANES = 128
SUBLANES = 8
VMEM_LIMIT = 56 * 1024 * 1024
S5_GROUPS_PER_BLOCK = 16

f32 = jnp.float32
bf16 = jnp.bfloat16


def _params(*sem):
    return pltpu.CompilerParams(dimension_semantics=sem, vmem_limit_bytes=VMEM_LIMIT)


def _rmsnorm(x, g):
    ms = jnp.mean(x * x, axis=-1, keepdims=True)
    return x * lax.rsqrt(ms + EPS) * g


CAST_BLOCK_BYTES = 6 * 1024 * 1024


def _cast_kernel(w_ref, o_ref):
    o_ref[...] = w_ref[...].astype(o_ref.dtype)


def _layer_bf16(w, layer):
    _, r, c = w.shape
    tr = r
    while tr % 2 == 0 and tr > SUBLANES and tr * c * w.dtype.itemsize > CAST_BLOCK_BYTES:
        tr //= 2
    return pl.pallas_call(
        _cast_kernel,
        out_shape=jax.ShapeDtypeStruct((r, c), bf16),
        grid=(r // tr,),
        in_specs=[pl.BlockSpec((None, tr, c), lambda i: (layer, i, 0))],
        out_specs=pl.BlockSpec((tr, c), lambda i: (i, 0)),
        compiler_params=_params("parallel"),
        name="layer_bf16",
    )(w)


def _ffn_kernel(x_ref, g_ref, wg_ref, wu_ref, wd_ref, o_ref, h_ref):
    @pl.when(pl.program_id(1) == 0)
    def _():
        x = x_ref[...]
        h_ref[...] = _rmsnorm(x, g_ref[...]).astype(bf16)
        o_ref[...] = x

    h = h_ref[...]
    gate = jnp.dot(h, wg_ref[...], preferred_element_type=f32)
    up = jnp.dot(h, wu_ref[...], preferred_element_type=f32)
    a = (gate * jax.nn.sigmoid(gate) * up * FFN_RES).astype(bf16)
    o_ref[...] += jnp.dot(a, wd_ref[...], preferred_element_type=f32)


def _ffn(x, g, wg, wu, wd, *, tm=1024, tf=512):
    n, d = x.shape
    f = wg.shape[1]
    tm, tf = min(tm, n), min(tf, f)
    assert n % tm == 0 and f % tf == 0
    return pl.pallas_call(
        _ffn_kernel,
        out_shape=jax.ShapeDtypeStruct((n, d), f32),
        grid=(n // tm, f // tf),
        in_specs=[
            pl.BlockSpec((tm, d), lambda i, j: (i, 0)),
            pl.BlockSpec((1, d), lambda i, j: (0, 0)),
            pl.BlockSpec((d, tf), lambda i, j: (0, j)),
            pl.BlockSpec((d, tf), lambda i, j: (0, j)),
            pl.BlockSpec((tf, d), lambda i, j: (j, 0)),
        ],
        out_specs=pl.BlockSpec((tm, d), lambda i, j: (i, 0)),
        scratch_shapes=[pltpu.VMEM((tm, d), bf16)],
        compiler_params=_params("parallel", "arbitrary"),
        name="ffn",
    )(x, g.reshape(1, d), wg, wu, wd)


def _norm_matmul_kernel(x_ref, g_ref, w_ref, o_ref, h_ref):
    @pl.when(pl.program_id(1) == 0)
    def _():
        h_ref[...] = _rmsnorm(x_ref[...], g_ref[...]).astype(bf16)

    o_ref[...] = jnp.dot(h_ref[...], w_ref[...], preferred_element_type=f32).astype(o_ref.dtype)


def _norm_matmul(x, g, w, *, out_dtype, tm=512, tn=2048):
    n, d = x.shape
    dout = w.shape[1]
    tm, tn = min(tm, n), min(tn, dout)
    assert n % tm == 0 and dout % tn == 0
    return pl.pallas_call(
        _norm_matmul_kernel,
        out_shape=jax.ShapeDtypeStruct((n, dout), out_dtype),
        grid=(n // tm, dout // tn),
        in_specs=[
            pl.BlockSpec((tm, d), lambda i, j: (i, 0)),
            pl.BlockSpec((1, d), lambda i, j: (0, 0)),
            pl.BlockSpec((d, tn), lambda i, j: (0, j)),
        ],
        out_specs=pl.BlockSpec((tm, tn), lambda i, j: (i, j)),
        scratch_shapes=[pltpu.VMEM((tm, d), bf16)],
        compiler_params=_params("parallel", "arbitrary"),
        name="norm_matmul",
    )(x, g.reshape(1, d), w)


def _qkv_kernel(x_ref, g_ref, w_ref, hg_ref, o_ref, h_ref, *, n_norm_tiles, head_dim):
    j = pl.program_id(1)

    @pl.when(j == 0)
    def _():
        h_ref[...] = _rmsnorm(x_ref[...], g_ref[...]).astype(bf16)

    acc = jnp.dot(h_ref[...], w_ref[...], preferred_element_type=f32)
    normed = j < n_norm_tiles
    hg = hg_ref[...]
    for c in range(0, acc.shape[1], head_dim):
        a = acc[:, c:c + head_dim]
        o_ref[:, c:c + head_dim] = jnp.where(normed, _rmsnorm(a, hg[:, c:c + head_dim]), a).astype(o_ref.dtype)


def _qkv(x, g, w, head_gain, n_norm_cols, head_dim, *, tm=1024, tn=1024):
    n, d = x.shape
    dout = w.shape[1]
    tm, tn = min(tm, n), min(tn, dout)
    assert n % tm == 0 and dout % tn == 0 and n_norm_cols % tn == 0 and tn % head_dim == 0
    n_norm_tiles = n_norm_cols // tn
    return pl.pallas_call(
        functools.partial(_qkv_kernel, n_norm_tiles=n_norm_tiles, head_dim=head_dim),
        out_shape=jax.ShapeDtypeStruct((n, dout), bf16),
        grid=(n // tm, dout // tn),
        in_specs=[
            pl.BlockSpec((tm, d), lambda i, j: (i, 0)),
            pl.BlockSpec((1, d), lambda i, j: (0, 0)),
            pl.BlockSpec((d, tn), lambda i, j: (0, j)),
            pl.BlockSpec((1, tn), lambda i, j: (0, jnp.minimum(j, n_norm_tiles - 1))),
        ],
        out_specs=pl.BlockSpec((tm, tn), lambda i, j: (i, j)),
        scratch_shapes=[pltpu.VMEM((tm, d), bf16)],
        compiler_params=_params("parallel", "arbitrary"),
        name="qkv",
    )(x, g.reshape(1, d), w, head_gain)


def _matmul_res_kernel(a_ref, w_ref, x_ref, o_ref):
    o_ref[...] = x_ref[...] + jnp.dot(a_ref[...], w_ref[...], preferred_element_type=f32)


def _matmul_res(a, w, x, *, tm=512, tn=2048):
    n, k = a.shape
    d = w.shape[1]
    tm, tn = min(tm, n), min(tn, d)
    assert n % tm == 0 and d % tn == 0
    return pl.pallas_call(
        _matmul_res_kernel,
        out_shape=jax.ShapeDtypeStruct((n, d), f32),
        grid=(n // tm, d // tn),
        in_specs=[
            pl.BlockSpec((tm, k), lambda i, j: (i, 0)),
            pl.BlockSpec((k, tn), lambda i, j: (0, j)),
            pl.BlockSpec((tm, tn), lambda i, j: (i, j)),
        ],
        out_specs=pl.BlockSpec((tm, tn), lambda i, j: (i, j)),
        compiler_params=_params("parallel", "arbitrary"),
        name="matmul_res",
    )(a, w, x)


def _glu_out_kernel(yb_ref, yj_ref, x_ref, wg_ref, b_ref, wo_ref, o_ref, acc_ref):
    slabs, ti, c = yb_ref.shape
    j = pl.program_id(1)

    @pl.when(j == 0)
    def _():
        acc_ref[...] = jnp.zeros_like(acc_ref)

    z = jnp.dot(yb_ref[...].reshape(slabs * ti, c), wg_ref[...], preferred_element_type=f32) + b_ref[...]
    a = (yj_ref[...].reshape(slabs * ti, -1) * jax.nn.sigmoid(z)).astype(bf16)
    acc_ref[...] += jnp.dot(a, wo_ref[...], preferred_element_type=f32)

    @pl.when(j == pl.num_programs(1) - 1)
    def _():
        for s in range(slabs):
            o_ref[:, s, :] = x_ref[:, s, :] + acc_ref[pl.ds(s * ti, ti), :]


def _glu_out(yb, y, x, w_glu, b_glu, w_out, *, ti=128, tn=512):
    slabs, ns, c = y.shape
    n = slabs * ns
    d = w_out.shape[1]
    ti, tn = min(ti, ns), min(tn, c)
    assert ns % ti == 0 and c % tn == 0
    out = pl.pallas_call(
        _glu_out_kernel,
        out_shape=jax.ShapeDtypeStruct((ns, slabs, d), f32),
        grid=(ns // ti, c // tn),
        in_specs=[
            pl.BlockSpec((slabs, ti, c), lambda i, j: (0, i, 0)),
            pl.BlockSpec((slabs, ti, tn), lambda i, j: (0, i, j)),
            pl.BlockSpec((ti, slabs, d), lambda i, j: (i, 0, 0), pipeline_mode=pl.Buffered(1)),
            pl.BlockSpec((c, tn), lambda i, j: (0, j)),
            pl.BlockSpec((1, tn), lambda i, j: (0, j)),
            pl.BlockSpec((tn, d), lambda i, j: (j, 0)),
        ],
        out_specs=pl.BlockSpec((ti, slabs, d), lambda i, j: (i, 0, 0)),
        scratch_shapes=[pltpu.VMEM((slabs * ti, d), f32)],
        compiler_params=_params("parallel", "arbitrary"),
        name="glu_out",
    )(yb, y, x.reshape(ns, slabs, d), w_glu, b_glu.reshape(1, c), w_out)
    return out.reshape(n, d)


S5_BLOCK = 8


def _s5_kernel(*refs, half, n_u):
    u_refs = refs[:n_u]
    wb_ref, kt_ref, wc_ref, lam_ref, d_ref, o_ref, ob_ref, uf_ref, ub_ref, s_ref, cb_ref, carry_ref = refs[n_u:]
    _, tr, cb = uf_ref.shape

    @pl.when(pl.program_id(2) == 0)
    def _():
        carry_ref[...] = jnp.zeros_like(carry_ref)

    for r in range(S5_BLOCK):
        for k in range(n_u):
            ur = u_refs[k][pl.ds(r, tr, stride=S5_BLOCK), :]
            uf_ref[r, :, pl.ds(k * LANES, LANES)] = ur
            ub_ref[:, pl.ds((S5_BLOCK - 1 - r) * cb + k * LANES, LANES)] = ur.astype(bf16)
    s_ref[...] = jnp.dot(ub_ref[...], wb_ref[0], preferred_element_type=f32)

    n_lane_tiles = half // LANES
    last = SUBLANES - 1
    first_row = lax.broadcasted_iota(jnp.int32, (SUBLANES, LANES), 0) == 0

    for j in range(S5_BLOCK):
        o_ref[j] = (jnp.dot(ub_ref[:, (S5_BLOCK - 1 - j) * cb:], kt_ref[0, :(j + 1) * cb, :],
                            preferred_element_type=f32) + d_ref[...] * uf_ref[j])

    def row_tile(i, carry):
        rows = pl.ds(i * SUBLANES, SUBLANES)
        out = []
        for c in range(n_lane_tiles):
            re_l = pl.ds(c * LANES, LANES)
            im_l = pl.ds(half + c * LANES, LANES)
            cre, cim = carry[2 * c], carry[2 * c + 1]
            xre = s_ref[rows, re_l]
            xim = s_ref[rows, im_l]
            for k, dist in enumerate((1, 2, 4)):
                lre = lam_ref[0, 2 * k, :, re_l]
                lim = lam_ref[0, 2 * k + 1, :, re_l]
                pre = pltpu.roll(xre, dist, 0)
                pim = pltpu.roll(xim, dist, 0)
                xre, xim = xre + (lre * pre - lim * pim), xim + (lre * pim + lim * pre)
            are = lam_ref[0, 6, :, re_l]
            aim = lam_ref[0, 7, :, re_l]
            xre = xre + (are * cre - aim * cim)
            xim = xim + (are * cim + aim * cre)
            s_ref[rows, re_l] = jnp.where(first_row, cre, pltpu.roll(xre, 1, 0))
            s_ref[rows, im_l] = jnp.where(first_row, cim, pltpu.roll(xim, 1, 0))
            out.append(jnp.broadcast_to(xre[last:last + 1, :], xre.shape))
            out.append(jnp.broadcast_to(xim[last:last + 1, :], xim.shape))
        return tuple(out)

    carry0 = []
    for c in range(n_lane_tiles):
        carry0.append(carry_ref[:, pl.ds(c * LANES, LANES)])
        carry0.append(carry_ref[:, pl.ds(half + c * LANES, LANES)])
    carry = tuple(carry0)
    for i in range(tr // SUBLANES):
        carry = row_tile(i, carry)
    for c in range(n_lane_tiles):
        carry_ref[:, pl.ds(c * LANES, LANES)] = carry[2 * c]
        carry_ref[:, pl.ds(half + c * LANES, LANES)] = carry[2 * c + 1]

    cb_ref[...] = s_ref[...].astype(bf16)
    for j in range(S5_BLOCK):
        y = jax.nn.gelu(o_ref[j] + jnp.dot(cb_ref[...], wc_ref[0, j], preferred_element_type=f32))
        o_ref[j] = y
        ob_ref[j] = y.astype(bf16)


def _s5_discretise(lam_re, lam_im, log_dt, b_re, b_im):
    dt = jnp.exp(log_dt.astype(f32))[:, None]
    lr = jnp.minimum(lam_re.astype(f32), S5_LAM_RE_MAX)
    li = lam_im.astype(f32)
    mag = jnp.exp(lr * dt)
    ab_re = mag * jnp.cos(li * dt)
    ab_im = mag * jnp.sin(li * dt)
    den = lr * lr + li * li
    n_re = ab_re - 1.0
    f_re = (n_re * lr + ab_im * li) / den
    f_im = (ab_im * lr - n_re * li) / den
    br = b_re.astype(f32)
    bi = b_im.astype(f32)
    bb_re = f_re[..., None] * br - f_im[..., None] * bi
    bb_im = f_re[..., None] * bi + f_im[..., None] * br
    return lr * dt, li * dt, bb_re, bb_im


def _expand_kernel(v_ref, e_ref, o_ref, *, row_group, col_group, groups, lhs_transposed):
    v = v_ref[0]
    if lhs_transposed:
        t = lax.dot_general(v, e_ref[...], (((0,), (0,)), ((), ())), preferred_element_type=f32)
    else:
        t = jnp.dot(v, e_ref[...], preferred_element_type=f32)
    rows = lax.broadcasted_iota(jnp.int32, t.shape, 0)
    cols = lax.broadcasted_iota(jnp.int32, t.shape, 1)
    row_g = lax.shift_right_logical(rows, row_group.bit_length() - 1) & (groups - 1)
    col_g = lax.shift_right_logical(cols, col_group.bit_length() - 1) & (groups - 1)
    o_ref[0] = jnp.where(row_g == col_g, t, 0.0).astype(o_ref.dtype)


def _expand_block_diag(v, e, *, row_group, col_group, groups, lhs_transposed=False):
    nlead = v.shape[0]
    rows = v.shape[2] if lhs_transposed else v.shape[1]
    cols = e.shape[1]
    for size in (row_group, col_group, groups):
        assert size & (size - 1) == 0
    return pl.pallas_call(
        functools.partial(_expand_kernel, row_group=row_group, col_group=col_group, groups=groups,
                          lhs_transposed=lhs_transposed),
        out_shape=jax.ShapeDtypeStruct((nlead, rows, cols), bf16),
        grid=(nlead,),
        in_specs=[pl.BlockSpec((1,) + v.shape[1:], lambda i: (i, 0, 0)),
                  pl.BlockSpec(e.shape, lambda i: (0, 0))],
        out_specs=pl.BlockSpec((1, rows, cols), lambda i: (i, 0, 0)),
        compiler_params=_params("parallel"),
        name="expand_block_diag",
    )(v, e)


def _s5_scan_consts(z_re, z_im, gpb):
    g, p = z_re.shape

    def power(k):
        m = jnp.exp(z_re * k)
        return m * jnp.cos(z_im * k), m * jnp.sin(z_im * k)

    sub = jnp.arange(SUBLANES)[:, None, None]
    rows = []
    for dist in (1, 2, 4):
        pr, pi = power(float(dist))
        rows.append(jnp.where(sub >= dist, pr[None], 0.0))
        rows.append(jnp.where(sub >= dist, pi[None], 0.0))
    ks = (jnp.arange(SUBLANES, dtype=f32) + 1.0)[:, None, None]
    m = jnp.exp(z_re[None] * ks)
    rows.append(m * jnp.cos(z_im[None] * ks))
    rows.append(m * jnp.sin(z_im[None] * ks))
    c = jnp.stack(rows)
    c = c.reshape(8, SUBLANES, g // gpb, gpb * p)
    return c.transpose(2, 0, 1, 3)


def _s5_core(u, lam_re, lam_im, log_dt, b_re, b_im, c_re, c_im, d_skip, *, batch, tr=256):
    n, ch = u.shape
    g, p, h = b_re.shape
    nblk = n // S5_BLOCK
    blk_per_seq = nblk // batch
    gpb = min(S5_GROUPS_PER_BLOCK, g)
    tr = min(tr, blk_per_seq)
    assert n % (S5_BLOCK * batch) == 0 and g % gpb == 0 and blk_per_seq % tr == 0
    assert tr % SUBLANES == 0 and (gpb * p) % LANES == 0 and (gpb * h) % LANES == 0
    nb = g // gpb
    half = gpb * p
    cb = gpb * h
    nt = blk_per_seq // tr
    z_re, z_im, bb_re, bb_im = _s5_discretise(lam_re, lam_im, log_dt, b_re, b_im)
    cr = c_re.astype(f32)
    ci = c_im.astype(f32)

    ks = jnp.arange(S5_BLOCK + 1, dtype=f32)[:, None, None]
    mag = jnp.exp(z_re[None] * ks)
    pw_re = mag * jnp.cos(z_im[None] * ks)
    pw_im = mag * jnp.sin(z_im[None] * ks)
    lb_re = pw_re[:-1, :, :, None] * bb_re[None] - pw_im[:-1, :, :, None] * bb_im[None]
    lb_im = pw_re[:-1, :, :, None] * bb_im[None] + pw_im[:-1, :, :, None] * bb_re[None]
    lane = jnp.arange(2 * half)
    col = jnp.arange(2 * p)
    rep_state = ((lane[None, :] // half == col[:, None] // p) & (lane[None, :] % p == col[:, None] % p)).astype(bf16)
    rep_chan = (jnp.arange(cb)[None, :] % h == jnp.arange(h)[:, None]).astype(bf16)
    wb = jnp.stack([lb_re, lb_im], axis=2)
    wb = wb.reshape(S5_BLOCK, nb, gpb, 2, p, h).transpose(1, 0, 2, 5, 3, 4)
    wb = _expand_block_diag(wb.reshape(nb * S5_BLOCK, cb, 2 * p).astype(bf16), rep_state,
                            row_group=h, col_group=p, groups=gpb).reshape(nb, S5_BLOCK * cb, 2 * half)
    kt = jnp.einsum("gop,tgpi->tgio", cr, lb_re) - jnp.einsum("gop,tgpi->tgio", ci, lb_im)
    kt = kt.reshape(S5_BLOCK, nb, gpb, h, h).transpose(1, 0, 2, 3, 4)
    kt = _expand_block_diag(kt.reshape(nb * S5_BLOCK, cb, h).astype(bf16), rep_chan,
                            row_group=h, col_group=h, groups=gpb).reshape(nb, S5_BLOCK * cb, cb)
    cl_re = cr[None] * pw_re[1:, :, None, :] - ci[None] * pw_im[1:, :, None, :]
    cl_im = cr[None] * pw_im[1:, :, None, :] + ci[None] * pw_re[1:, :, None, :]
    wc = jnp.stack([cl_re, -cl_im], axis=2)
    wc = wc.reshape(S5_BLOCK, nb, gpb, 2, h, p).transpose(1, 0, 4, 3, 2, 5)
    wc = _expand_block_diag(wc.reshape(nb * S5_BLOCK, h, 2 * half).astype(bf16), rep_chan, row_group=p,
                            col_group=h, groups=gpb, lhs_transposed=True).reshape(nb, S5_BLOCK, 2 * half, cb)
    lam = _s5_scan_consts(z_re * S5_BLOCK, z_im * S5_BLOCK, gpb)

    once = dict(pipeline_mode=pl.Buffered(1))
    n_u = cb // LANES
    u_specs = [pl.BlockSpec((S5_BLOCK * tr, LANES), lambda j, b, t, k=k: (b * nt + t, j * n_u + k))
               for k in range(n_u)]
    out_spec = pl.BlockSpec((S5_BLOCK, tr, cb), lambda j, b, t: (0, b * nt + t, j))
    return pl.pallas_call(
        functools.partial(_s5_kernel, half=half, n_u=n_u),
        out_shape=(jax.ShapeDtypeStruct((S5_BLOCK, nblk, ch), f32),
                   jax.ShapeDtypeStruct((S5_BLOCK, nblk, ch), bf16)),
        grid=(nb, batch, nt),
        in_specs=u_specs + [
            pl.BlockSpec((1, S5_BLOCK * cb, 2 * half), lambda j, b, t: (j, 0, 0), **once),
            pl.BlockSpec((1, S5_BLOCK * cb, cb), lambda j, b, t: (j, 0, 0), **once),
            pl.BlockSpec((1, S5_BLOCK, 2 * half, cb), lambda j, b, t: (j, 0, 0, 0), **once),
            pl.BlockSpec((1, 8, SUBLANES, half), lambda j, b, t: (j, 0, 0, 0)),
            pl.BlockSpec((1, cb), lambda j, b, t: (0, j)),
        ],
        out_specs=(out_spec, out_spec),
        scratch_shapes=[pltpu.VMEM((S5_BLOCK, tr, cb), f32),
                        pltpu.VMEM((tr, S5_BLOCK * cb), bf16), pltpu.VMEM((tr, 2 * half), f32),
                        pltpu.VMEM((tr, 2 * half), bf16), pltpu.VMEM((SUBLANES, 2 * half), f32)],
        compiler_params=_params("parallel", "parallel", "arbitrary"),
        name="s5_core",
    )(*([u] * n_u), wb, kt, wc, lam, d_skip.reshape(1, ch).astype(f32))


def _sb_kernel(q_ref, k_ref, v_ref, m_ref, o_ref, acc_ref, carry_ref, z0_ref, z1_ref, w0_ref, w1_ref,
               *, heads_per_block):
    t = q_ref.shape[0]
    nsub = t // LANES
    qi = pl.program_id(2)
    m = m_ref[...]
    row = lax.broadcasted_iota(jnp.int32, (t, t), 0)
    col = lax.broadcasted_iota(jnp.int32, (t, t), 1)
    past = col < row
    acc_ref[...] = jnp.zeros_like(acc_ref)
    carry_ref[...] = jnp.zeros_like(carry_ref)

    def key_rows(kb):
        return pl.ds(pl.multiple_of(kb * t, t), t)

    all_heads = range(heads_per_block)

    def scores(kb, z_ref, heads=all_heads):
        ks = key_rows(kb)
        for hh in heads:
            z_ref[:, pl.ds(hh * t, t)] = lax.dot_general(
                q_ref[:, pl.ds(hh * LANES, LANES)], k_ref[ks, pl.ds(hh * LANES, LANES)],
                (((1,), (1,)), ((), ())), preferred_element_type=f32)

    def weights(z_ref, w_ref, heads=all_heads, masked=False):
        for hh in heads:
            hl = pl.ds(hh * LANES, LANES)
            z = z_ref[:, pl.ds(hh * t, t)]
            zneg = jnp.minimum(z, 0.0)
            zdiff = zneg - z
            soft = jnp.log(1.0 + jnp.exp2(zdiff + zneg)) * LOG2E
            log_beta = zneg - soft
            log_keep = zdiff - soft
            if masked:
                log_keep = jnp.where(past, log_keep, 0.0)
            run = carry_ref[:, hl]
            for j in reversed(range(nsub)):
                sl = slice(j * LANES, (j + 1) * LANES)
                lk = log_keep[:, sl]
                hi = lk.astype(bf16)
                lo = (lk - hi.astype(f32)).astype(bf16)
                cs = jnp.dot(jnp.concatenate([hi, lo], axis=1), m, preferred_element_type=f32)
                w = jnp.exp2(log_beta[:, sl] + cs[:, :LANES] + run)
                if masked:
                    w = jnp.where(past[:, sl], w, 0.0)
                w_ref[:, pl.ds(hh * t + j * LANES, LANES)] = w.astype(bf16)
                run = run + cs[:, LANES:]
            carry_ref[:, hl] = run

    def weighted_values(kb, w_ref, heads=all_heads):
        ks = key_rows(kb)
        for hh in heads:
            hl = pl.ds(hh * LANES, LANES)
            acc_ref[:, hl] += jnp.dot(w_ref[:, pl.ds(hh * t, t)], v_ref[ks, hl], preferred_element_type=f32)

    for hh in all_heads:
        scores(qi, z0_ref, (hh,))
        scores(jnp.maximum(qi - 1, 0), z1_ref, (hh,))
    for hh in all_heads:
        weights(z0_ref, w0_ref, (hh,), masked=True)

    def step(kb, z_ref, w_ref, z_next_ref, w_prev_ref):
        for hh in all_heads:
            weighted_values(kb + 1, w_prev_ref, (hh,))
            weights(z_ref, w_ref, (hh,))
            scores(jnp.maximum(kb - 1, 0), z_next_ref, (hh,))

    def two_steps(p, c):
        kb = qi - 1 - 2 * p
        step(kb, z1_ref, w1_ref, z0_ref, w0_ref)
        step(kb - 1, z0_ref, w0_ref, z1_ref, w1_ref)
        return c

    lax.fori_loop(0, qi // 2, two_steps, 0)

    @pl.when(qi % 2 == 1)
    def _():
        for hh in all_heads:
            weighted_values(1, w0_ref, (hh,))
            weights(z1_ref, w1_ref, (hh,))
        weighted_values(0, w1_ref)

    @pl.when(qi % 2 == 0)
    def _():
        weighted_values(0, w0_ref)

    o_ref[...] = acc_ref[...].astype(o_ref.dtype)


def _sb_attention(qkv, *, batch, heads, head_dim, t=256, heads_per_block=4):
    n = qkv.shape[0]
    seq = n // batch
    t = min(t, seq)
    hpb = min(heads_per_block, heads)
    assert seq % t == 0 and t % LANES == 0 and head_dim == LANES and heads % hpb == 0
    nq = seq // t
    nhb = heads // hpb
    wb = hpb * head_dim
    tri = (jnp.arange(LANES)[:, None] > jnp.arange(LANES)[None, :]).astype(bf16)
    m = jnp.concatenate([tri, jnp.ones((LANES, LANES), bf16)], axis=1)
    m = jnp.concatenate([m, m], axis=0)
    return pl.pallas_call(
        functools.partial(_sb_kernel, heads_per_block=hpb),
        out_shape=jax.ShapeDtypeStruct((n, heads * head_dim), bf16),
        grid=(batch, nhb, nq),
        in_specs=[
            pl.BlockSpec((t, wb), lambda b, h, i: (b * nq + i, h)),
            pl.BlockSpec((seq, wb), lambda b, h, i: (b, nhb + h)),
            pl.BlockSpec((seq, wb), lambda b, h, i: (b, 2 * nhb + h)),
            pl.BlockSpec((2 * LANES, 2 * LANES), lambda b, h, i: (0, 0)),
        ],
        out_specs=pl.BlockSpec((t, wb), lambda b, h, i: (b * nq + i, h)),
        scratch_shapes=[pltpu.VMEM((t, wb), f32), pltpu.VMEM((t, wb), f32),
                        pltpu.VMEM((t, hpb * t), f32), pltpu.VMEM((t, hpb * t), f32),
                        pltpu.VMEM((t, hpb * t), bf16), pltpu.VMEM((t, hpb * t), bf16)],
        compiler_params=_params("parallel", "parallel", "arbitrary"),
        name="sb_attention",
    )(qkv, qkv, qkv, m)


def kernel(x, norm_ffn1, ffn1_w_gate, ffn1_w_up, ffn1_w_down, norm_mix, s5_w_in, s5_lam_re, s5_lam_im, s5_log_dt, s5_b_re, s5_b_im, s5_c_re, s5_c_im, s5_d, s5_w_glu, s5_b_glu, s5_w_out, sb_w_qkv, sb_g_q, sb_g_k, sb_w_o, norm_ffn2, ffn2_w_gate, ffn2_w_up, ffn2_w_down):
    batch, seq, d = x.shape
    depth = norm_ffn1.shape[0]
    head_dim = sb_g_q.shape[1]
    heads = sb_w_o.shape[1] // head_dim
    n_mixers = 2
    xf = x.reshape(batch * seq, d).astype(f32)
    for i in range(depth):
        xf = _ffn(xf, norm_ffn1[i], _layer_bf16(ffn1_w_gate, i), _layer_bf16(ffn1_w_up, i),
                  _layer_bf16(ffn1_w_down, i))
        j = i // n_mixers
        if i % n_mixers == 0:
            u = _norm_matmul(xf, norm_mix[i], _layer_bf16(s5_w_in, j), out_dtype=f32)
            y, yb = _s5_core(u, s5_lam_re[j], s5_lam_im[j], s5_log_dt[j], s5_b_re[j], s5_b_im[j],
                             s5_c_re[j], s5_c_im[j], s5_d[j], batch=batch)
            xf = _glu_out(yb, y, xf, _layer_bf16(s5_w_glu, j), s5_b_glu[j].astype(f32), _layer_bf16(s5_w_out, j))
        else:
            hd = heads * head_dim
            q_scale = LOG2E / math.sqrt(head_dim)
            head_gain = jnp.concatenate([jnp.tile(sb_g_q[j].astype(f32) * q_scale, heads),
                                         jnp.tile(sb_g_k[j].astype(f32), heads)]).reshape(1, 2 * hd)
            qkv = _qkv(xf, norm_mix[i], _layer_bf16(sb_w_qkv, j), head_gain, 2 * hd, head_dim)
            o = _sb_attention(qkv, batch=batch, heads=heads, head_dim=head_dim)
            xf = _matmul_res(o, _layer_bf16(sb_w_o, j), xf)
        xf = _ffn(xf, norm_ffn2[i], _layer_bf16(ffn2_w_gate, i), _layer_bf16(ffn2_w_up, i),
                  _layer_bf16(ffn2_w_down, i))
    return xf.reshape(batch, seq, d).astype(x.dtype)
```

```python
import functools
import math

import jax
import jax.numpy as jnp
from jax import lax
from jax.experimental import pallas as pl
from jax.experimental.pallas import tpu as pltpu

EPS = 1e-6
FFN_RES = 0.5
S5_LAM_RE_MAX = -1e-4
LOG2E = 1.4426950408889634
LANES = 128
SUBLANES = 8
VMEM_LIMIT = 56 * 1024 * 1024
S5_GROUPS_PER_BLOCK = 16

f32 = jnp.float32
bf16 = jnp.bfloat16


def _params(*sem):
    return pltpu.CompilerParams(dimension_semantics=sem, vmem_limit_bytes=VMEM_LIMIT)


def _rmsnorm(x, g):
    ms = jnp.mean(x * x, axis=-1, keepdims=True)
    return x * lax.rsqrt(ms + EPS) * g


CAST_BLOCK_BYTES = 12 * 1024 * 1024


def _cast_kernel(w_ref, o_ref):
    o_ref[...] = w_ref[...].astype(o_ref.dtype)


def _layer_bf16(w, layer):
    _, r, c = w.shape
    tr = r
    while tr % 2 == 0 and tr > SUBLANES and tr * c * w.dtype.itemsize > CAST_BLOCK_BYTES:
        tr //= 2
    return pl.pallas_call(
        _cast_kernel,
        out_shape=jax.ShapeDtypeStruct((r, c), bf16),
        grid=(r // tr,),
        in_specs=[pl.BlockSpec((None, tr, c), lambda i: (layer, i, 0))],
        out_specs=pl.BlockSpec((tr, c), lambda i: (i, 0)),
        compiler_params=_params("parallel"),
        name="layer_bf16",
    )(w)


def _ffn_kernel(x_ref, g_ref, wg_ref, wu_ref, wd_ref, o_ref, h_ref):
    @pl.when(pl.program_id(1) == 0)
    def _():
        x = x_ref[...]
        h_ref[...] = _rmsnorm(x, g_ref[...]).astype(bf16)
        o_ref[...] = x

    h = h_ref[...]
    gate = jnp.dot(h, wg_ref[...], preferred_element_type=f32)
    up = jnp.dot(h, wu_ref[...], preferred_element_type=f32)
    a = (gate * jax.nn.sigmoid(gate) * up * FFN_RES).astype(bf16)
    o_ref[...] += jnp.dot(a, wd_ref[...], preferred_element_type=f32)


def _ffn(x, g, wg, wu, wd, *, tm=1024, tf=512):
    n, d = x.shape
    f = wg.shape[1]
    tm, tf = min(tm, n), min(tf, f)
    assert n % tm == 0 and f % tf == 0
    return pl.pallas_call(
        _ffn_kernel,
        out_shape=jax.ShapeDtypeStruct((n, d), f32),
        grid=(n // tm, f // tf),
        in_specs=[
            pl.BlockSpec((tm, d), lambda i, j: (i, 0)),
            pl.BlockSpec((1, d), lambda i, j: (0, 0)),
            pl.BlockSpec((d, tf), lambda i, j: (0, j)),
            pl.BlockSpec((d, tf), lambda i, j: (0, j)),
            pl.BlockSpec((tf, d), lambda i, j: (j, 0)),
        ],
        out_specs=pl.BlockSpec((tm, d), lambda i, j: (i, 0)),
        scratch_shapes=[pltpu.VMEM((tm, d), bf16)],
        compiler_params=_params("parallel", "arbitrary"),
        name="ffn",
    )(x, g.reshape(1, d), wg, wu, wd)


def _norm_matmul_kernel(x_ref, g_ref, w_ref, o_ref, h_ref):
    @pl.when(pl.program_id(1) == 0)
    def _():
        h_ref[...] = _rmsnorm(x_ref[...], g_ref[...]).astype(bf16)

    o_ref[...] = jnp.dot(h_ref[...], w_ref[...], preferred_element_type=f32).astype(o_ref.dtype)


def _norm_matmul(x, g, w, *, out_dtype, tm=512, tn=2048):
    n, d = x.shape
    dout = w.shape[1]
    tm, tn = min(tm, n), min(tn, dout)
    assert n % tm == 0 and dout % tn == 0
    return pl.pallas_call(
        _norm_matmul_kernel,
        out_shape=jax.ShapeDtypeStruct((n, dout), out_dtype),
        grid=(n // tm, dout // tn),
        in_specs=[
            pl.BlockSpec((tm, d), lambda i, j: (i, 0)),
            pl.BlockSpec((1, d), lambda i, j: (0, 0)),
            pl.BlockSpec((d, tn), lambda i, j: (0, j)),
        ],
        out_specs=pl.BlockSpec((tm, tn), lambda i, j: (i, j)),
        scratch_shapes=[pltpu.VMEM((tm, d), bf16)],
        compiler_params=_params("parallel", "arbitrary"),
        name="norm_matmul",
    )(x, g.reshape(1, d), w)


def _qkv_kernel(x_ref, g_ref, w_ref, hg_ref, o_ref, h_ref, *, n_norm_tiles, head_dim):
    j = pl.program_id(1)

    @pl.when(j == 0)
    def _():
        h_ref[...] = _rmsnorm(x_ref[...], g_ref[...]).astype(bf16)

    acc = jnp.dot(h_ref[...], w_ref[...], preferred_element_type=f32)
    normed = j < n_norm_tiles
    hg = hg_ref[...]
    for c in range(0, acc.shape[1], head_dim):
        a = acc[:, c:c + head_dim]
        o_ref[:, c:c + head_dim] = jnp.where(normed, _rmsnorm(a, hg[:, c:c + head_dim]), a).astype(o_ref.dtype)


def _qkv(x, g, w, head_gain, n_norm_cols, head_dim, *, tm=1024, tn=1024):
    n, d = x.shape
    dout = w.shape[1]
    tm, tn = min(tm, n), min(tn, dout)
    assert n % tm == 0 and dout % tn == 0 and n_norm_cols % tn == 0 and tn % head_dim == 0
    n_norm_tiles = n_norm_cols // tn
    return pl.pallas_call(
        functools.partial(_qkv_kernel, n_norm_tiles=n_norm_tiles, head_dim=head_dim),
        out_shape=jax.ShapeDtypeStruct((n, dout), bf16),
        grid=(n // tm, dout // tn),
        in_specs=[
            pl.BlockSpec((tm, d), lambda i, j: (i, 0)),
            pl.BlockSpec((1, d), lambda i, j: (0, 0)),
            pl.BlockSpec((d, tn), lambda i, j: (0, j)),
            pl.BlockSpec((1, tn), lambda i, j: (0, jnp.minimum(j, n_norm_tiles - 1))),
        ],
        out_specs=pl.BlockSpec((tm, tn), lambda i, j: (i, j)),
        scratch_shapes=[pltpu.VMEM((tm, d), bf16)],
        compiler_params=_params("parallel", "arbitrary"),
        name="qkv",
    )(x, g.reshape(1, d), w, head_gain)


def _matmul_res_kernel(a_ref, w_ref, x_ref, o_ref):
    o_ref[...] = x_ref[...] + jnp.dot(a_ref[...], w_ref[...], preferred_element_type=f32)


def _matmul_res(a, w, x, *, tm=512, tn=2048):
    n, k = a.shape
    d = w.shape[1]
    tm, tn = min(tm, n), min(tn, d)
    assert n % tm == 0 and d % tn == 0
    return pl.pallas_call(
        _matmul_res_kernel,
        out_shape=jax.ShapeDtypeStruct((n, d), f32),
        grid=(n // tm, d // tn),
        in_specs=[
            pl.BlockSpec((tm, k), lambda i, j: (i, 0)),
            pl.BlockSpec((k, tn), lambda i, j: (0, j)),
            pl.BlockSpec((tm, tn), lambda i, j: (i, j)),
        ],
        out_specs=pl.BlockSpec((tm, tn), lambda i, j: (i, j)),
        compiler_params=_params("parallel", "arbitrary"),
        name="matmul_res",
    )(a, w, x)


def _glu_out_kernel(yb_ref, yj_ref, x_ref, wg_ref, b_ref, wo_ref, o_ref, acc_ref):
    slabs, ti, c = yb_ref.shape
    j = pl.program_id(1)

    @pl.when(j == 0)
    def _():
        acc_ref[...] = jnp.zeros_like(acc_ref)

    z = jnp.dot(yb_ref[...].reshape(slabs * ti, c), wg_ref[...], preferred_element_type=f32) + b_ref[...]
    a = (yj_ref[...].reshape(slabs * ti, -1) * jax.nn.sigmoid(z)).astype(bf16)
    acc_ref[...] += jnp.dot(a, wo_ref[...], preferred_element_type=f32)

    @pl.when(j == pl.num_programs(1) - 1)
    def _():
        for s in range(slabs):
            o_ref[:, s, :] = x_ref[:, s, :] + acc_ref[pl.ds(s * ti, ti), :]


def _glu_out(yb, y, x, w_glu, b_glu, w_out, *, ti=128, tn=512):
    slabs, ns, c = y.shape
    n = slabs * ns
    d = w_out.shape[1]
    ti, tn = min(ti, ns), min(tn, c)
    assert ns % ti == 0 and c % tn == 0
    out = pl.pallas_call(
        _glu_out_kernel,
        out_shape=jax.ShapeDtypeStruct((ns, slabs, d), f32),
        grid=(ns // ti, c // tn),
        in_specs=[
            pl.BlockSpec((slabs, ti, c), lambda i, j: (0, i, 0)),
            pl.BlockSpec((slabs, ti, tn), lambda i, j: (0, i, j)),
            pl.BlockSpec((ti, slabs, d), lambda i, j: (i, 0, 0), pipeline_mode=pl.Buffered(1)),
            pl.BlockSpec((c, tn), lambda i, j: (0, j)),
            pl.BlockSpec((1, tn), lambda i, j: (0, j)),
            pl.BlockSpec((tn, d), lambda i, j: (j, 0)),
        ],
        out_specs=pl.BlockSpec((ti, slabs, d), lambda i, j: (i, 0, 0)),
        scratch_shapes=[pltpu.VMEM((slabs * ti, d), f32)],
        compiler_params=_params("parallel", "arbitrary"),
        name="glu_out",
    )(yb, y, x.reshape(ns, slabs, d), w_glu, b_glu.reshape(1, c), w_out)
    return out.reshape(n, d)


S5_BLOCK = 8


def _s5_kernel(*refs, half, n_u):
    u_refs = refs[:n_u]
    wb_ref, kt_ref, wc_ref, lam_ref, d_ref, o_ref, ob_ref, uf_ref, ub_ref, s_ref, cb_ref, carry_ref = refs[n_u:]
    _, tr, cb = uf_ref.shape

    @pl.when(pl.program_id(2) == 0)
    def _():
        carry_ref[...] = jnp.zeros_like(carry_ref)

    for r in range(S5_BLOCK):
        for k in range(n_u):
            ur = u_refs[k][pl.ds(r, tr, stride=S5_BLOCK), :]
            uf_ref[r, :, pl.ds(k * LANES, LANES)] = ur
            ub_ref[:, pl.ds((S5_BLOCK - 1 - r) * cb + k * LANES, LANES)] = ur.astype(bf16)
    s_ref[...] = jnp.dot(ub_ref[...], wb_ref[0], preferred_element_type=f32)

    n_lane_tiles = half // LANES
    last = SUBLANES - 1
    first_row = lax.broadcasted_iota(jnp.int32, (SUBLANES, LANES), 0) == 0

    for j in range(S5_BLOCK):
        o_ref[j] = (jnp.dot(ub_ref[:, (S5_BLOCK - 1 - j) * cb:], kt_ref[0, :(j + 1) * cb, :],
                            preferred_element_type=f32) + d_ref[...] * uf_ref[j])

    def row_tile(i, carry):
        rows = pl.ds(i * SUBLANES, SUBLANES)
        out = []
        for c in range(n_lane_tiles):
            re_l = pl.ds(c * LANES, LANES)
            im_l = pl.ds(half + c * LANES, LANES)
            cre, cim = carry[2 * c], carry[2 * c + 1]
            xre = s_ref[rows, re_l]
            xim = s_ref[rows, im_l]
            for k, dist in enumerate((1, 2, 4)):
                lre = lam_ref[0, 2 * k, :, re_l]
                lim = lam_ref[0, 2 * k + 1, :, re_l]
                pre = pltpu.roll(xre, dist, 0)
                pim = pltpu.roll(xim, dist, 0)
                xre, xim = xre + (lre * pre - lim * pim), xim + (lre * pim + lim * pre)
            are = lam_ref[0, 6, :, re_l]
            aim = lam_ref[0, 7, :, re_l]
            xre = xre + (are * cre - aim * cim)
            xim = xim + (are * cim + aim * cre)
            s_ref[rows, re_l] = jnp.where(first_row, cre, pltpu.roll(xre, 1, 0))
            s_ref[rows, im_l] = jnp.where(first_row, cim, pltpu.roll(xim, 1, 0))
            out.append(jnp.broadcast_to(xre[last:last + 1, :], xre.shape))
            out.append(jnp.broadcast_to(xim[last:last + 1, :], xim.shape))
        return tuple(out)

    carry0 = []
    for c in range(n_lane_tiles):
        carry0.append(carry_ref[:, pl.ds(c * LANES, LANES)])
        carry0.append(carry_ref[:, pl.ds(half + c * LANES, LANES)])
    carry = tuple(carry0)
    for i in range(tr // SUBLANES):
        carry = row_tile(i, carry)
    for c in range(n_lane_tiles):
        carry_ref[:, pl.ds(c * LANES, LANES)] = carry[2 * c]
        carry_ref[:, pl.ds(half + c * LANES, LANES)] = carry[2 * c + 1]

    cb_ref[...] = s_ref[...].astype(bf16)
    for j in range(S5_BLOCK):
        y = jax.nn.gelu(o_ref[j] + jnp.dot(cb_ref[...], wc_ref[0, j], preferred_element_type=f32))
        o_ref[j] = y
        ob_ref[j] = y.astype(bf16)


def _s5_discretise(lam_re, lam_im, log_dt, b_re, b_im):
    dt = jnp.exp(log_dt.astype(f32))[:, None]
    lr = jnp.minimum(lam_re.astype(f32), S5_LAM_RE_MAX)
    li = lam_im.astype(f32)
    mag = jnp.exp(lr * dt)
    ab_re = mag * jnp.cos(li * dt)
    ab_im = mag * jnp.sin(li * dt)
    den = lr * lr + li * li
    n_re = ab_re - 1.0
    f_re = (n_re * lr + ab_im * li) / den
    f_im = (ab_im * lr - n_re * li) / den
    br = b_re.astype(f32)
    bi = b_im.astype(f32)
    bb_re = f_re[..., None] * br - f_im[..., None] * bi
    bb_im = f_re[..., None] * bi + f_im[..., None] * br
    return lr * dt, li * dt, bb_re, bb_im


EXPAND_PER_STEP = 8


def _expand_kernel(v_ref, e_ref, o_ref, *, row_group, col_group, groups, lhs_transposed):
    rows = lax.broadcasted_iota(jnp.int32, o_ref.shape[1:], 0)
    cols = lax.broadcasted_iota(jnp.int32, o_ref.shape[1:], 1)
    row_g = lax.shift_right_logical(rows, row_group.bit_length() - 1) & (groups - 1)
    col_g = lax.shift_right_logical(cols, col_group.bit_length() - 1) & (groups - 1)
    same_group = row_g == col_g
    for i in range(v_ref.shape[0]):
        if lhs_transposed:
            t = lax.dot_general(v_ref[i], e_ref[...], (((0,), (0,)), ((), ())), preferred_element_type=f32)
        else:
            t = jnp.dot(v_ref[i], e_ref[...], preferred_element_type=f32)
        o_ref[i] = jnp.where(same_group, t, 0.0).astype(o_ref.dtype)


def _expand_block_diag(v, e, *, row_group, col_group, groups, lhs_transposed=False):
    nlead = v.shape[0]
    rows = v.shape[2] if lhs_transposed else v.shape[1]
    cols = e.shape[1]
    for size in (row_group, col_group, groups):
        assert size & (size - 1) == 0
    per_step = math.gcd(nlead, EXPAND_PER_STEP)
    return pl.pallas_call(
        functools.partial(_expand_kernel, row_group=row_group, col_group=col_group, groups=groups,
                          lhs_transposed=lhs_transposed),
        out_shape=jax.ShapeDtypeStruct((nlead, rows, cols), bf16),
        grid=(nlead // per_step,),
        in_specs=[pl.BlockSpec((per_step,) + v.shape[1:], lambda i: (i, 0, 0)),
                  pl.BlockSpec(e.shape, lambda i: (0, 0))],
        out_specs=pl.BlockSpec((per_step, rows, cols), lambda i: (i, 0, 0)),
        compiler_params=_params("parallel"),
        name="expand_block_diag",
    )(v, e)


def _s5_scan_consts(z_re, z_im, gpb):
    g, p = z_re.shape

    def power(k):
        m = jnp.exp(z_re * k)
        return m * jnp.cos(z_im * k), m * jnp.sin(z_im * k)

    sub = jnp.arange(SUBLANES)[:, None, None]
    rows = []
    for dist in (1, 2, 4):
        pr, pi = power(float(dist))
        rows.append(jnp.where(sub >= dist, pr[None], 0.0))
        rows.append(jnp.where(sub >= dist, pi[None], 0.0))
    ks = (jnp.arange(SUBLANES, dtype=f32) + 1.0)[:, None, None]
    m = jnp.exp(z_re[None] * ks)
    rows.append(m * jnp.cos(z_im[None] * ks))
    rows.append(m * jnp.sin(z_im[None] * ks))
    c = jnp.stack(rows)
    c = c.reshape(8, SUBLANES, g // gpb, gpb * p)
    return c.transpose(2, 0, 1, 3)


def _s5_core(u, lam_re, lam_im, log_dt, b_re, b_im, c_re, c_im, d_skip, *, batch, tr=256):
    n, ch = u.shape
    g, p, h = b_re.shape
    nblk = n // S5_BLOCK
    blk_per_seq = nblk // batch
    gpb = min(S5_GROUPS_PER_BLOCK, g)
    tr = min(tr, blk_per_seq)
    assert n % (S5_BLOCK * batch) == 0 and g % gpb == 0 and blk_per_seq % tr == 0
    assert tr % SUBLANES == 0 and (gpb * p) % LANES == 0 and (gpb * h) % LANES == 0
    nb = g // gpb
    half = gpb * p
    cb = gpb * h
    nt = blk_per_seq // tr
    z_re, z_im, bb_re, bb_im = _s5_discretise(lam_re, lam_im, log_dt, b_re, b_im)
    cr = c_re.astype(f32)
    ci = c_im.astype(f32)

    ks = jnp.arange(S5_BLOCK + 1, dtype=f32)[:, None, None]
    mag = jnp.exp(z_re[None] * ks)
    pw_re = mag * jnp.cos(z_im[None] * ks)
    pw_im = mag * jnp.sin(z_im[None] * ks)
    lb_re = pw_re[:-1, :, :, None] * bb_re[None] - pw_im[:-1, :, :, None] * bb_im[None]
    lb_im = pw_re[:-1, :, :, None] * bb_im[None] + pw_im[:-1, :, :, None] * bb_re[None]
    lane = jnp.arange(2 * half)
    col = jnp.arange(2 * p)
    rep_state = ((lane[None, :] // half == col[:, None] // p) & (lane[None, :] % p == col[:, None] % p)).astype(bf16)
    rep_chan = (jnp.arange(cb)[None, :] % h == jnp.arange(h)[:, None]).astype(bf16)
    wb = jnp.stack([lb_re, lb_im], axis=2)
    wb = wb.reshape(S5_BLOCK, nb, gpb, 2, p, h).transpose(1, 0, 2, 5, 3, 4)
    wb = _expand_block_diag(wb.reshape(nb * S5_BLOCK, cb, 2 * p).astype(bf16), rep_state,
                            row_group=h, col_group=p, groups=gpb).reshape(nb, S5_BLOCK * cb, 2 * half)
    kt = jnp.einsum("gop,tgpi->tgio", cr, lb_re) - jnp.einsum("gop,tgpi->tgio", ci, lb_im)
    kt = kt.reshape(S5_BLOCK, nb, gpb, h, h).transpose(1, 0, 2, 3, 4)
    kt = _expand_block_diag(kt.reshape(nb * S5_BLOCK, cb, h).astype(bf16), rep_chan,
                            row_group=h, col_group=h, groups=gpb).reshape(nb, S5_BLOCK * cb, cb)
    cl_re = cr[None] * pw_re[1:, :, None, :] - ci[None] * pw_im[1:, :, None, :]
    cl_im = cr[None] * pw_im[1:, :, None, :] + ci[None] * pw_re[1:, :, None, :]
    wc = jnp.stack([cl_re, -cl_im], axis=2)
    wc = wc.reshape(S5_BLOCK, nb, gpb, 2, h, p).transpose(1, 0, 4, 3, 2, 5)
    wc = _expand_block_diag(wc.reshape(nb * S5_BLOCK, h, 2 * half).astype(bf16), rep_chan, row_group=p,
                            col_group=h, groups=gpb, lhs_transposed=True).reshape(nb, S5_BLOCK, 2 * half, cb)
    lam = _s5_scan_consts(z_re * S5_BLOCK, z_im * S5_BLOCK, gpb)

    once = dict(pipeline_mode=pl.Buffered(1))
    n_u = cb // LANES
    u_specs = [pl.BlockSpec((S5_BLOCK * tr, LANES), lambda j, b, t, k=k: (b * nt + t, j * n_u + k))
               for k in range(n_u)]
    out_spec = pl.BlockSpec((S5_BLOCK, tr, cb), lambda j, b, t: (0, b * nt + t, j))
    return pl.pallas_call(
        functools.partial(_s5_kernel, half=half, n_u=n_u),
        out_shape=(jax.ShapeDtypeStruct((S5_BLOCK, nblk, ch), f32),
                   jax.ShapeDtypeStruct((S5_BLOCK, nblk, ch), bf16)),
        grid=(nb, batch, nt),
        in_specs=u_specs + [
            pl.BlockSpec((1, S5_BLOCK * cb, 2 * half), lambda j, b, t: (j, 0, 0), **once),
            pl.BlockSpec((1, S5_BLOCK * cb, cb), lambda j, b, t: (j, 0, 0), **once),
            pl.BlockSpec((1, S5_BLOCK, 2 * half, cb), lambda j, b, t: (j, 0, 0, 0), **once),
            pl.BlockSpec((1, 8, SUBLANES, half), lambda j, b, t: (j, 0, 0, 0)),
            pl.BlockSpec((1, cb), lambda j, b, t: (0, j)),
        ],
        out_specs=(out_spec, out_spec),
        scratch_shapes=[pltpu.VMEM((S5_BLOCK, tr, cb), f32),
                        pltpu.VMEM((tr, S5_BLOCK * cb), bf16), pltpu.VMEM((tr, 2 * half), f32),
                        pltpu.VMEM((tr, 2 * half), bf16), pltpu.VMEM((SUBLANES, 2 * half), f32)],
        compiler_params=_params("parallel", "parallel", "arbitrary"),
        name="s5_core",
    )(*([u] * n_u), wb, kt, wc, lam, d_skip.reshape(1, ch).astype(f32))


def _sb_kernel(q_ref, k_ref, v_ref, m_ref, o_ref, acc_ref, carry_ref, z0_ref, z1_ref, w0_ref, w1_ref,
               *, heads_per_block):
    t = q_ref.shape[0]
    nsub = t // LANES
    qi = pl.program_id(2)
    m = m_ref[...]
    row = lax.broadcasted_iota(jnp.int32, (t, t), 0)
    col = lax.broadcasted_iota(jnp.int32, (t, t), 1)
    past = col < row
    acc_ref[...] = jnp.zeros_like(acc_ref)
    carry_ref[...] = jnp.zeros_like(carry_ref)

    def key_rows(kb):
        return pl.ds(pl.multiple_of(kb * t, t), t)

    all_heads = range(heads_per_block)

    def scores(kb, z_ref, heads=all_heads):
        ks = key_rows(kb)
        for hh in heads:
            z_ref[:, pl.ds(hh * t, t)] = lax.dot_general(
                q_ref[:, pl.ds(hh * LANES, LANES)], k_ref[ks, pl.ds(hh * LANES, LANES)],
                (((1,), (1,)), ((), ())), preferred_element_type=f32)

    def weights(z_ref, w_ref, heads=all_heads, masked=False):
        for hh in heads:
            hl = pl.ds(hh * LANES, LANES)
            z = z_ref[:, pl.ds(hh * t, t)]
            zneg = jnp.minimum(z, 0.0)
            zdiff = zneg - z
            soft = jnp.log(1.0 + jnp.exp2(zdiff + zneg)) * LOG2E
            log_beta = zneg - soft
            log_keep = zdiff - soft
            if masked:
                log_keep = jnp.where(past, log_keep, 0.0)
            run = carry_ref[:, hl]
            for j in reversed(range(nsub)):
                sl = slice(j * LANES, (j + 1) * LANES)
                lk = log_keep[:, sl]
                hi = lk.astype(bf16)
                lo = (lk - hi.astype(f32)).astype(bf16)
                cs = jnp.dot(jnp.concatenate([hi, lo], axis=1), m, preferred_element_type=f32)
                w = jnp.exp2(log_beta[:, sl] + cs[:, :LANES] + run)
                if masked:
                    w = jnp.where(past[:, sl], w, 0.0)
                w_ref[:, pl.ds(hh * t + j * LANES, LANES)] = w.astype(bf16)
                run = run + cs[:, LANES:]
            carry_ref[:, hl] = run

    def weighted_values(kb, w_ref, heads=all_heads):
        ks = key_rows(kb)
        for hh in heads:
            hl = pl.ds(hh * LANES, LANES)
            acc_ref[:, hl] += jnp.dot(w_ref[:, pl.ds(hh * t, t)], v_ref[ks, hl], preferred_element_type=f32)

    for hh in all_heads:
        scores(qi, z0_ref, (hh,))
        scores(jnp.maximum(qi - 1, 0), z1_ref, (hh,))
    for hh in all_heads:
        weights(z0_ref, w0_ref, (hh,), masked=True)

    def step(kb, z_ref, w_ref, z_next_ref, w_prev_ref):
        for hh in all_heads:
            weighted_values(kb + 1, w_prev_ref, (hh,))
            weights(z_ref, w_ref, (hh,))
            scores(jnp.maximum(kb - 1, 0), z_next_ref, (hh,))

    def two_steps(p, c):
        kb = qi - 1 - 2 * p
        step(kb, z1_ref, w1_ref, z0_ref, w0_ref)
        step(kb - 1, z0_ref, w0_ref, z1_ref, w1_ref)
        return c

    lax.fori_loop(0, qi // 2, two_steps, 0)

    @pl.when(qi % 2 == 1)
    def _():
        for hh in all_heads:
            weighted_values(1, w0_ref, (hh,))
            weights(z1_ref, w1_ref, (hh,))
        weighted_values(0, w1_ref)

    @pl.when(qi % 2 == 0)
    def _():
        weighted_values(0, w0_ref)

    o_ref[...] = acc_ref[...].astype(o_ref.dtype)


def _sb_attention(qkv, *, batch, heads, head_dim, t=256, heads_per_block=8):
    n = qkv.shape[0]
    seq = n // batch
    t = min(t, seq)
    hpb = min(heads_per_block, heads)
    assert seq % t == 0 and t % LANES == 0 and head_dim == LANES and heads % hpb == 0
    nq = seq // t
    nhb = heads // hpb
    wb = hpb * head_dim
    tri = (jnp.arange(LANES)[:, None] > jnp.arange(LANES)[None, :]).astype(bf16)
    m = jnp.concatenate([tri, jnp.ones((LANES, LANES), bf16)], axis=1)
    m = jnp.concatenate([m, m], axis=0)
    return pl.pallas_call(
        functools.partial(_sb_kernel, heads_per_block=hpb),
        out_shape=jax.ShapeDtypeStruct((n, heads * head_dim), bf16),
        grid=(batch, nhb, nq),
        in_specs=[
            pl.BlockSpec((t, wb), lambda b, h, i: (b * nq + i, h)),
            pl.BlockSpec((seq, wb), lambda b, h, i: (b, nhb + h)),
            pl.BlockSpec((seq, wb), lambda b, h, i: (b, 2 * nhb + h)),
            pl.BlockSpec((2 * LANES, 2 * LANES), lambda b, h, i: (0, 0)),
        ],
        out_specs=pl.BlockSpec((t, wb), lambda b, h, i: (b * nq + i, h)),
        scratch_shapes=[pltpu.VMEM((t, wb), f32), pltpu.VMEM((t, wb), f32),
                        pltpu.VMEM((t, hpb * t), f32), pltpu.VMEM((t, hpb * t), f32),
                        pltpu.VMEM((t, hpb * t), bf16), pltpu.VMEM((t, hpb * t), bf16)],
        compiler_params=_params("parallel", "parallel", "arbitrary"),
        name="sb_attention",
    )(qkv, qkv, qkv, m)


def kernel(x, norm_ffn1, ffn1_w_gate, ffn1_w_up, ffn1_w_down, norm_mix, s5_w_in, s5_lam_re, s5_lam_im, s5_log_dt, s5_b_re, s5_b_im, s5_c_re, s5_c_im, s5_d, s5_w_glu, s5_b_glu, s5_w_out, sb_w_qkv, sb_g_q, sb_g_k, sb_w_o, norm_ffn2, ffn2_w_gate, ffn2_w_up, ffn2_w_down):
    batch, seq, d = x.shape
    depth = norm_ffn1.shape[0]
    head_dim = sb_g_q.shape[1]
    heads = sb_w_o.shape[1] // head_dim
    n_mixers = 2
    xf = x.reshape(batch * seq, d).astype(f32)
    for i in range(depth):
        xf = _ffn(xf, norm_ffn1[i], _layer_bf16(ffn1_w_gate, i), _layer_bf16(ffn1_w_up, i),
                  _layer_bf16(ffn1_w_down, i))
        j = i // n_mixers
        if i % n_mixers == 0:
            u = _norm_matmul(xf, norm_mix[i], _layer_bf16(s5_w_in, j), out_dtype=f32)
            y, yb = _s5_core(u, s5_lam_re[j], s5_lam_im[j], s5_log_dt[j], s5_b_re[j], s5_b_im[j],
                             s5_c_re[j], s5_c_im[j], s5_d[j], batch=batch)
            xf = _glu_out(yb, y, xf, _layer_bf16(s5_w_glu, j), s5_b_glu[j].astype(f32), _layer_bf16(s5_w_out, j))
        else:
            hd = heads * head_dim
            q_scale = LOG2E / math.sqrt(head_dim)
            head_gain = jnp.concatenate([jnp.tile(sb_g_q[j].astype(f32) * q_scale, heads),
                                         jnp.tile(sb_g_k[j].astype(f32), heads)]).reshape(1, 2 * hd)
            qkv = _qkv(xf, norm_mix[i], _layer_bf16(sb_w_qkv, j), head_gain, 2 * hd, head_dim)
            o = _sb_attention(qkv, batch=batch, heads=heads, head_dim=head_dim)
            xf = _matmul_res(o, _layer_bf16(sb_w_o, j), xf)
        xf = _ffn(xf, norm_ffn2[i], _layer_bf16(ffn2_w_gate, i), _layer_bf16(ffn2_w_up, i),
                  _layer_bf16(ffn2_w_down, i))
    return xf.reshape(batch, seq, d).astype(x.dtype)
```

```python
import functools
import math

import jax
import jax.numpy as jnp
from jax import lax
from jax.experimental import pallas as pl
from jax.experimental.pallas import tpu as pltpu

EPS = 1e-6
FFN_RES = 0.5
S5_LAM_RE_MAX = -1e-4
LOG2E = 1.4426950408889634
LANES = 128
SUBLANES = 8
VMEM_LIMIT = 56 * 1024 * 1024
S5_GROUPS_PER_BLOCK = 16

f32 = jnp.float32
bf16 = jnp.bfloat16


def _params(*sem):
    return pltpu.CompilerParams(dimension_semantics=sem, vmem_limit_bytes=VMEM_LIMIT)


def _rmsnorm(x, g):
    ms = jnp.mean(x * x, axis=-1, keepdims=True)
    return x * lax.rsqrt(ms + EPS) * g


CAST_BLOCK_BYTES = 12 * 1024 * 1024


def _cast_kernel(w_ref, o_ref):
    o_ref[...] = w_ref[...].astype(o_ref.dtype)


def _layer_bf16(w, layer):
    _, r, c = w.shape
    tr = r
    while tr % 2 == 0 and tr > SUBLANES and tr * c * w.dtype.itemsize > CAST_BLOCK_BYTES:
        tr //= 2
    return pl.pallas_call(
        _cast_kernel,
        out_shape=jax.ShapeDtypeStruct((r, c), bf16),
        grid=(r // tr,),
        in_specs=[pl.BlockSpec((None, tr, c), lambda i: (layer, i, 0))],
        out_specs=pl.BlockSpec((tr, c), lambda i: (i, 0)),
        compiler_params=_params("parallel"),
        name="layer_bf16",
    )(w)


def _ffn_kernel(x_ref, g_ref, wg_ref, wu_ref, wd_ref, o_ref, h_ref):
    @pl.when(pl.program_id(1) == 0)
    def _():
        x = x_ref[...]
        h_ref[...] = _rmsnorm(x, g_ref[...]).astype(bf16)
        o_ref[...] = x

    h = h_ref[...]
    gate = jnp.dot(h, wg_ref[...], preferred_element_type=f32)
    up = jnp.dot(h, wu_ref[...], preferred_element_type=f32)
    a = (gate * jax.nn.sigmoid(gate) * up * FFN_RES).astype(bf16)
    o_ref[...] += jnp.dot(a, wd_ref[...], preferred_element_type=f32)


def _ffn(x, g, wg, wu, wd, *, tm=1024, tf=512):
    n, d = x.shape
    f = wg.shape[1]
    tm, tf = min(tm, n), min(tf, f)
    assert n % tm == 0 and f % tf == 0
    return pl.pallas_call(
        _ffn_kernel,
        out_shape=jax.ShapeDtypeStruct((n, d), f32),
        grid=(n // tm, f // tf),
        in_specs=[
            pl.BlockSpec((tm, d), lambda i, j: (i, 0)),
            pl.BlockSpec((1, d), lambda i, j: (0, 0)),
            pl.BlockSpec((d, tf), lambda i, j: (0, j)),
            pl.BlockSpec((d, tf), lambda i, j: (0, j)),
            pl.BlockSpec((tf, d), lambda i, j: (j, 0)),
        ],
        out_specs=pl.BlockSpec((tm, d), lambda i, j: (i, 0)),
        scratch_shapes=[pltpu.VMEM((tm, d), bf16)],
        compiler_params=_params("parallel", "arbitrary"),
        name="ffn",
    )(x, g.reshape(1, d), wg, wu, wd)


def _norm_matmul_kernel(x_ref, g_ref, w_ref, o_ref, h_ref):
    @pl.when(pl.program_id(1) == 0)
    def _():
        h_ref[...] = _rmsnorm(x_ref[...], g_ref[...]).astype(bf16)

    o_ref[...] = jnp.dot(h_ref[...], w_ref[...], preferred_element_type=f32).astype(o_ref.dtype)


def _norm_matmul(x, g, w, *, out_dtype, tm=512, tn=2048):
    n, d = x.shape
    dout = w.shape[1]
    tm, tn = min(tm, n), min(tn, dout)
    assert n % tm == 0 and dout % tn == 0
    return pl.pallas_call(
        _norm_matmul_kernel,
        out_shape=jax.ShapeDtypeStruct((n, dout), out_dtype),
        grid=(n // tm, dout // tn),
        in_specs=[
            pl.BlockSpec((tm, d), lambda i, j: (i, 0)),
            pl.BlockSpec((1, d), lambda i, j: (0, 0)),
            pl.BlockSpec((d, tn), lambda i, j: (0, j)),
        ],
        out_specs=pl.BlockSpec((tm, tn), lambda i, j: (i, j)),
        scratch_shapes=[pltpu.VMEM((tm, d), bf16)],
        compiler_params=_params("parallel", "arbitrary"),
        name="norm_matmul",
    )(x, g.reshape(1, d), w)


def _qkv_kernel(x_ref, g_ref, w_ref, hg_ref, o_ref, h_ref, *, n_norm_tiles, head_dim):
    j = pl.program_id(1)

    @pl.when(j == 0)
    def _():
        h_ref[...] = _rmsnorm(x_ref[...], g_ref[...]).astype(bf16)

    acc = jnp.dot(h_ref[...], w_ref[...], preferred_element_type=f32)
    normed = j < n_norm_tiles
    hg = hg_ref[...]
    for c in range(0, acc.shape[1], head_dim):
        a = acc[:, c:c + head_dim]
        o_ref[:, c:c + head_dim] = jnp.where(normed, _rmsnorm(a, hg[:, c:c + head_dim]), a).astype(o_ref.dtype)


def _qkv(x, g, w, head_gain, n_norm_cols, head_dim, *, tm=1024, tn=1024):
    n, d = x.shape
    dout = w.shape[1]
    tm, tn = min(tm, n), min(tn, dout)
    assert n % tm == 0 and dout % tn == 0 and n_norm_cols % tn == 0 and tn % head_dim == 0
    n_norm_tiles = n_norm_cols // tn
    return pl.pallas_call(
        functools.partial(_qkv_kernel, n_norm_tiles=n_norm_tiles, head_dim=head_dim),
        out_shape=jax.ShapeDtypeStruct((n, dout), bf16),
        grid=(n // tm, dout // tn),
        in_specs=[
            pl.BlockSpec((tm, d), lambda i, j: (i, 0)),
            pl.BlockSpec((1, d), lambda i, j: (0, 0)),
            pl.BlockSpec((d, tn), lambda i, j: (0, j)),
            pl.BlockSpec((1, tn), lambda i, j: (0, jnp.minimum(j, n_norm_tiles - 1))),
        ],
        out_specs=pl.BlockSpec((tm, tn), lambda i, j: (i, j)),
        scratch_shapes=[pltpu.VMEM((tm, d), bf16)],
        compiler_params=_params("parallel", "arbitrary"),
        name="qkv",
    )(x, g.reshape(1, d), w, head_gain)


def _matmul_res_kernel(a_ref, w_ref, x_ref, o_ref):
    o_ref[...] = x_ref[...] + jnp.dot(a_ref[...], w_ref[...], preferred_element_type=f32)


def _matmul_res(a, w, x, *, tm=512, tn=2048):
    n, k = a.shape
    d = w.shape[1]
    tm, tn = min(tm, n), min(tn, d)
    assert n % tm == 0 and d % tn == 0
    return pl.pallas_call(
        _matmul_res_kernel,
        out_shape=jax.ShapeDtypeStruct((n, d), f32),
        grid=(n // tm, d // tn),
        in_specs=[
            pl.BlockSpec((tm, k), lambda i, j: (i, 0)),
            pl.BlockSpec((k, tn), lambda i, j: (0, j)),
            pl.BlockSpec((tm, tn), lambda i, j: (i, j)),
        ],
        out_specs=pl.BlockSpec((tm, tn), lambda i, j: (i, j)),
        compiler_params=_params("parallel", "arbitrary"),
        name="matmul_res",
    )(a, w, x)


def _glu_out_kernel(yb_ref, yj_ref, x_ref, wg_ref, b_ref, wo_ref, o_ref, acc_ref):
    slabs, ti, c = yb_ref.shape
    j = pl.program_id(1)

    @pl.when(j == 0)
    def _():
        acc_ref[...] = jnp.zeros_like(acc_ref)

    z = jnp.dot(yb_ref[...].reshape(slabs * ti, c), wg_ref[...], preferred_element_type=f32) + b_ref[...]
    a = (yj_ref[...].reshape(slabs * ti, -1) * jax.nn.sigmoid(z)).astype(bf16)
    acc_ref[...] += jnp.dot(a, wo_ref[...], preferred_element_type=f32)

    @pl.when(j == pl.num_programs(1) - 1)
    def _():
        for s in range(slabs):
            o_ref[:, s, :] = x_ref[:, s, :] + acc_ref[pl.ds(s * ti, ti), :]


def _glu_out(yb, y, x, w_glu, b_glu, w_out, *, ti=128, tn=512):
    slabs, ns, c = y.shape
    n = slabs * ns
    d = w_out.shape[1]
    ti, tn = min(ti, ns), min(tn, c)
    assert ns % ti == 0 and c % tn == 0
    out = pl.pallas_call(
        _glu_out_kernel,
        out_shape=jax.ShapeDtypeStruct((ns, slabs, d), f32),
        grid=(ns // ti, c // tn),
        in_specs=[
            pl.BlockSpec((slabs, ti, c), lambda i, j: (0, i, 0)),
            pl.BlockSpec((slabs, ti, tn), lambda i, j: (0, i, j)),
            pl.BlockSpec((ti, slabs, d), lambda i, j: (i, 0, 0), pipeline_mode=pl.Buffered(1)),
            pl.BlockSpec((c, tn), lambda i, j: (0, j)),
            pl.BlockSpec((1, tn), lambda i, j: (0, j)),
            pl.BlockSpec((tn, d), lambda i, j: (j, 0)),
        ],
        out_specs=pl.BlockSpec((ti, slabs, d), lambda i, j: (i, 0, 0)),
        scratch_shapes=[pltpu.VMEM((slabs * ti, d), f32)],
        compiler_params=_params("parallel", "arbitrary"),
        name="glu_out",
    )(yb, y, x.reshape(ns, slabs, d), w_glu, b_glu.reshape(1, c), w_out)
    return out.reshape(n, d)


S5_BLOCK = 8


def _s5_kernel(*refs, half, n_u):
    u_refs = refs[:n_u]
    wb_ref, kt_ref, wc_ref, lam_ref, d_ref, o_ref, ob_ref, uf_ref, ub_ref, s_ref, cb_ref, carry_ref = refs[n_u:]
    _, tr, cb = uf_ref.shape

    @pl.when(pl.program_id(2) == 0)
    def _():
        carry_ref[...] = jnp.zeros_like(carry_ref)

    for r in range(S5_BLOCK):
        for k in range(n_u):
            ur = u_refs[k][pl.ds(r, tr, stride=S5_BLOCK), :]
            uf_ref[r, :, pl.ds(k * LANES, LANES)] = ur
            ub_ref[:, pl.ds((S5_BLOCK - 1 - r) * cb + k * LANES, LANES)] = ur.astype(bf16)
    s_ref[...] = jnp.dot(ub_ref[...], wb_ref[0], preferred_element_type=f32)

    n_lane_tiles = half // LANES
    last = SUBLANES - 1
    first_row = lax.broadcasted_iota(jnp.int32, (SUBLANES, LANES), 0) == 0

    for j in range(S5_BLOCK):
        o_ref[j] = (jnp.dot(ub_ref[:, (S5_BLOCK - 1 - j) * cb:], kt_ref[0, :(j + 1) * cb, :],
                            preferred_element_type=f32) + d_ref[...] * uf_ref[j])

    def row_tile(i, carry):
        rows = pl.ds(i * SUBLANES, SUBLANES)
        out = []
        for c in range(n_lane_tiles):
            re_l = pl.ds(c * LANES, LANES)
            im_l = pl.ds(half + c * LANES, LANES)
            cre, cim = carry[2 * c], carry[2 * c + 1]
            xre = s_ref[rows, re_l]
            xim = s_ref[rows, im_l]
            for k, dist in enumerate((1, 2, 4)):
                lre = lam_ref[0, 2 * k, :, re_l]
                lim = lam_ref[0, 2 * k + 1, :, re_l]
                pre = pltpu.roll(xre, dist, 0)
                pim = pltpu.roll(xim, dist, 0)
                xre, xim = xre + (lre * pre - lim * pim), xim + (lre * pim + lim * pre)
            are = lam_ref[0, 6, :, re_l]
            aim = lam_ref[0, 7, :, re_l]
            xre = xre + (are * cre - aim * cim)
            xim = xim + (are * cim + aim * cre)
            s_ref[rows, re_l] = jnp.where(first_row, cre, pltpu.roll(xre, 1, 0))
            s_ref[rows, im_l] = jnp.where(first_row, cim, pltpu.roll(xim, 1, 0))
            out.append(jnp.broadcast_to(xre[last:last + 1, :], xre.shape))
            out.append(jnp.broadcast_to(xim[last:last + 1, :], xim.shape))
        return tuple(out)

    carry0 = []
    for c in range(n_lane_tiles):
        carry0.append(carry_ref[:, pl.ds(c * LANES, LANES)])
        carry0.append(carry_ref[:, pl.ds(half + c * LANES, LANES)])
    carry = tuple(carry0)
    for i in range(tr // SUBLANES):
        carry = row_tile(i, carry)
    for c in range(n_lane_tiles):
        carry_ref[:, pl.ds(c * LANES, LANES)] = carry[2 * c]
        carry_ref[:, pl.ds(half + c * LANES, LANES)] = carry[2 * c + 1]

    cb_ref[...] = s_ref[...].astype(bf16)
    for j in range(S5_BLOCK):
        y = jax.nn.gelu(o_ref[j] + jnp.dot(cb_ref[...], wc_ref[0, j], preferred_element_type=f32))
        o_ref[j] = y
        ob_ref[j] = y.astype(bf16)


def _s5_discretise(lam_re, lam_im, log_dt, b_re, b_im):
    dt = jnp.exp(log_dt.astype(f32))[:, None]
    lr = jnp.minimum(lam_re.astype(f32), S5_LAM_RE_MAX)
    li = lam_im.astype(f32)
    mag = jnp.exp(lr * dt)
    ab_re = mag * jnp.cos(li * dt)
    ab_im = mag * jnp.sin(li * dt)
    den = lr * lr + li * li
    n_re = ab_re - 1.0
    f_re = (n_re * lr + ab_im * li) / den
    f_im = (ab_im * lr - n_re * li) / den
    br = b_re.astype(f32)
    bi = b_im.astype(f32)
    bb_re = f_re[..., None] * br - f_im[..., None] * bi
    bb_im = f_re[..., None] * bi + f_im[..., None] * br
    return lr * dt, li * dt, bb_re, bb_im


EXPAND_PER_STEP = 8


def _expand_kernel(v_ref, e_ref, o_ref, *, row_group, col_group, groups, lhs_transposed):
    rows = lax.broadcasted_iota(jnp.int32, o_ref.shape[1:], 0)
    cols = lax.broadcasted_iota(jnp.int32, o_ref.shape[1:], 1)
    row_g = lax.shift_right_logical(rows, row_group.bit_length() - 1) & (groups - 1)
    col_g = lax.shift_right_logical(cols, col_group.bit_length() - 1) & (groups - 1)
    same_group = row_g == col_g
    for i in range(v_ref.shape[0]):
        if lhs_transposed:
            t = lax.dot_general(v_ref[i], e_ref[...], (((0,), (0,)), ((), ())), preferred_element_type=f32)
        else:
            t = jnp.dot(v_ref[i], e_ref[...], preferred_element_type=f32)
        o_ref[i] = jnp.where(same_group, t, 0.0).astype(o_ref.dtype)


def _expand_block_diag(v, e, *, row_group, col_group, groups, lhs_transposed=False):
    nlead = v.shape[0]
    rows = v.shape[2] if lhs_transposed else v.shape[1]
    cols = e.shape[1]
    for size in (row_group, col_group, groups):
        assert size & (size - 1) == 0
    per_step = math.gcd(nlead, EXPAND_PER_STEP)
    return pl.pallas_call(
        functools.partial(_expand_kernel, row_group=row_group, col_group=col_group, groups=groups,
                          lhs_transposed=lhs_transposed),
        out_shape=jax.ShapeDtypeStruct((nlead, rows, cols), bf16),
        grid=(nlead // per_step,),
        in_specs=[pl.BlockSpec((per_step,) + v.shape[1:], lambda i: (i, 0, 0)),
                  pl.BlockSpec(e.shape, lambda i: (0, 0))],
        out_specs=pl.BlockSpec((per_step, rows, cols), lambda i: (i, 0, 0)),
        compiler_params=_params("parallel"),
        name="expand_block_diag",
    )(v, e)


def _s5_scan_consts(z_re, z_im, gpb):
    g, p = z_re.shape

    def power(k):
        m = jnp.exp(z_re * k)
        return m * jnp.cos(z_im * k), m * jnp.sin(z_im * k)

    sub = jnp.arange(SUBLANES)[:, None, None]
    rows = []
    for dist in (1, 2, 4):
        pr, pi = power(float(dist))
        rows.append(jnp.where(sub >= dist, pr[None], 0.0))
        rows.append(jnp.where(sub >= dist, pi[None], 0.0))
    ks = (jnp.arange(SUBLANES, dtype=f32) + 1.0)[:, None, None]
    m = jnp.exp(z_re[None] * ks)
    rows.append(m * jnp.cos(z_im[None] * ks))
    rows.append(m * jnp.sin(z_im[None] * ks))
    c = jnp.stack(rows)
    c = c.reshape(8, SUBLANES, g // gpb, gpb * p)
    return c.transpose(2, 0, 1, 3)


def _s5_core(u, lam_re, lam_im, log_dt, b_re, b_im, c_re, c_im, d_skip, *, batch, tr=256):
    n, ch = u.shape
    g, p, h = b_re.shape
    nblk = n // S5_BLOCK
    blk_per_seq = nblk // batch
    gpb = min(S5_GROUPS_PER_BLOCK, g)
    tr = min(tr, blk_per_seq)
    assert n % (S5_BLOCK * batch) == 0 and g % gpb == 0 and blk_per_seq % tr == 0
    assert tr % SUBLANES == 0 and (gpb * p) % LANES == 0 and (gpb * h) % LANES == 0
    nb = g // gpb
    half = gpb * p
    cb = gpb * h
    nt = blk_per_seq // tr
    z_re, z_im, bb_re, bb_im = _s5_discretise(lam_re, lam_im, log_dt, b_re, b_im)
    cr = c_re.astype(f32)
    ci = c_im.astype(f32)

    ks = jnp.arange(S5_BLOCK + 1, dtype=f32)[:, None, None]
    mag = jnp.exp(z_re[None] * ks)
    pw_re = mag * jnp.cos(z_im[None] * ks)
    pw_im = mag * jnp.sin(z_im[None] * ks)
    lb_re = pw_re[:-1, :, :, None] * bb_re[None] - pw_im[:-1, :, :, None] * bb_im[None]
    lb_im = pw_re[:-1, :, :, None] * bb_im[None] + pw_im[:-1, :, :, None] * bb_re[None]
    lane = jnp.arange(2 * half)
    col = jnp.arange(2 * p)
    rep_state = ((lane[None, :] // half == col[:, None] // p) & (lane[None, :] % p == col[:, None] % p)).astype(bf16)
    rep_chan = (jnp.arange(cb)[None, :] % h == jnp.arange(h)[:, None]).astype(bf16)
    wb = jnp.stack([lb_re, lb_im], axis=2)
    wb = wb.reshape(S5_BLOCK, nb, gpb, 2, p, h).transpose(1, 0, 2, 5, 3, 4)
    wb = _expand_block_diag(wb.reshape(nb * S5_BLOCK, cb, 2 * p).astype(bf16), rep_state,
                            row_group=h, col_group=p, groups=gpb).reshape(nb, S5_BLOCK * cb, 2 * half)
    kt = jnp.einsum("gop,tgpi->tgio", cr, lb_re) - jnp.einsum("gop,tgpi->tgio", ci, lb_im)
    kt = kt.reshape(S5_BLOCK, nb, gpb, h, h).transpose(1, 0, 2, 3, 4)
    kt = _expand_block_diag(kt.reshape(nb * S5_BLOCK, cb, h).astype(bf16), rep_chan,
                            row_group=h, col_group=h, groups=gpb).reshape(nb, S5_BLOCK * cb, cb)
    cl_re = cr[None] * pw_re[1:, :, None, :] - ci[None] * pw_im[1:, :, None, :]
    cl_im = cr[None] * pw_im[1:, :, None, :] + ci[None] * pw_re[1:, :, None, :]
    wc = jnp.stack([cl_re, -cl_im], axis=2)
    wc = wc.reshape(S5_BLOCK, nb, gpb, 2, h, p).transpose(1, 0, 4, 3, 2, 5)
    wc = _expand_block_diag(wc.reshape(nb * S5_BLOCK, h, 2 * half).astype(bf16), rep_chan, row_group=p,
                            col_group=h, groups=gpb, lhs_transposed=True).reshape(nb, S5_BLOCK, 2 * half, cb)
    lam = _s5_scan_consts(z_re * S5_BLOCK, z_im * S5_BLOCK, gpb)

    once = dict(pipeline_mode=pl.Buffered(1))
    n_u = cb // LANES
    u_specs = [pl.BlockSpec((S5_BLOCK * tr, LANES), lambda j, b, t, k=k: (b * nt + t, j * n_u + k))
               for k in range(n_u)]
    out_spec = pl.BlockSpec((S5_BLOCK, tr, cb), lambda j, b, t: (0, b * nt + t, j))
    return pl.pallas_call(
        functools.partial(_s5_kernel, half=half, n_u=n_u),
        out_shape=(jax.ShapeDtypeStruct((S5_BLOCK, nblk, ch), f32),
                   jax.ShapeDtypeStruct((S5_BLOCK, nblk, ch), bf16)),
        grid=(nb, batch, nt),
        in_specs=u_specs + [
            pl.BlockSpec((1, S5_BLOCK * cb, 2 * half), lambda j, b, t: (j, 0, 0), **once),
            pl.BlockSpec((1, S5_BLOCK * cb, cb), lambda j, b, t: (j, 0, 0), **once),
            pl.BlockSpec((1, S5_BLOCK, 2 * half, cb), lambda j, b, t: (j, 0, 0, 0), **once),
            pl.BlockSpec((1, 8, SUBLANES, half), lambda j, b, t: (j, 0, 0, 0)),
            pl.BlockSpec((1, cb), lambda j, b, t: (0, j)),
        ],
        out_specs=(out_spec, out_spec),
        scratch_shapes=[pltpu.VMEM((S5_BLOCK, tr, cb), f32),
                        pltpu.VMEM((tr, S5_BLOCK * cb), bf16), pltpu.VMEM((tr, 2 * half), f32),
                        pltpu.VMEM((tr, 2 * half), bf16), pltpu.VMEM((SUBLANES, 2 * half), f32)],
        compiler_params=_params("parallel", "parallel", "arbitrary"),
        name="s5_core",
    )(*([u] * n_u), wb, kt, wc, lam, d_skip.reshape(1, ch).astype(f32))


def _sb_kernel(q_ref, k_ref, v_ref, m_ref, o_ref, acc_ref, carry_ref, z0_ref, z1_ref, w0_ref, w1_ref,
               *, heads_per_block):
    t = q_ref.shape[0]
    nsub = t // LANES
    qi = pl.program_id(2)
    m = m_ref[...]
    row = lax.broadcasted_iota(jnp.int32, (t, t), 0)
    col = lax.broadcasted_iota(jnp.int32, (t, t), 1)
    past = col < row
    acc_ref[...] = jnp.zeros_like(acc_ref)
    carry_ref[...] = jnp.zeros_like(carry_ref)

    def key_rows(kb):
        return pl.ds(pl.multiple_of(kb * t, t), t)

    all_heads = range(heads_per_block)

    def scores(kb, z_ref, heads=all_heads):
        ks = key_rows(kb)
        for hh in heads:
            z_ref[:, pl.ds(hh * t, t)] = lax.dot_general(
                q_ref[:, pl.ds(hh * LANES, LANES)], k_ref[ks, pl.ds(hh * LANES, LANES)],
                (((1,), (1,)), ((), ())), preferred_element_type=f32)

    def block_weights(z, run, mask):
        zneg = jnp.minimum(z, 0.0)
        zdiff = zneg - z
        soft = jnp.log(1.0 + jnp.exp2(zdiff + zneg)) * LOG2E
        log_beta = zneg - soft
        log_keep = zdiff - soft
        if mask is not None:
            log_keep = jnp.where(mask, log_keep, 0.0)
        hi = log_keep.astype(bf16)
        lo = (log_keep - hi.astype(f32)).astype(bf16)
        cs = jnp.dot(jnp.concatenate([hi, lo], axis=1), m, preferred_element_type=f32)
        w = jnp.exp2(log_beta + cs[:, :LANES] + run)
        if mask is not None:
            w = jnp.where(mask, w, 0.0)
        return w.astype(bf16), cs[:, LANES:]

    def weights(z_ref, w_ref, heads=all_heads):
        for hh in heads:
            hl = pl.ds(hh * LANES, LANES)
            run = carry_ref[:, hl]
            for j in reversed(range(nsub)):
                cols = pl.ds(hh * t + j * LANES, LANES)
                w, row_sum = block_weights(z_ref[:, cols], run, None)
                w_ref[:, cols] = w
                run = run + row_sum
            carry_ref[:, hl] = run

    def diagonal_weights(z_ref, w_ref, heads=all_heads):
        diag = past[:LANES, :LANES]
        for hh in heads:
            hl = pl.ds(hh * LANES, LANES)
            for i in range(nsub):
                rows = pl.ds(i * LANES, LANES)
                run = carry_ref[rows, hl]
                for j in reversed(range(nsub)):
                    cols = pl.ds(hh * t + j * LANES, LANES)
                    if j > i:
                        w_ref[rows, cols] = jnp.zeros((LANES, LANES), bf16)
                        continue
                    w, row_sum = block_weights(z_ref[rows, cols], run, diag if j == i else None)
                    w_ref[rows, cols] = w
                    run = run + row_sum
                carry_ref[rows, hl] = run

    def weighted_values(kb, w_ref, heads=all_heads):
        ks = key_rows(kb)
        for hh in heads:
            hl = pl.ds(hh * LANES, LANES)
            acc_ref[:, hl] += jnp.dot(w_ref[:, pl.ds(hh * t, t)], v_ref[ks, hl], preferred_element_type=f32)

    for hh in all_heads:
        scores(qi, z0_ref, (hh,))
        scores(jnp.maximum(qi - 1, 0), z1_ref, (hh,))
    for hh in all_heads:
        diagonal_weights(z0_ref, w0_ref, (hh,))

    def step(kb, z_ref, w_ref, z_next_ref, w_prev_ref):
        for hh in all_heads:
            weighted_values(kb + 1, w_prev_ref, (hh,))
            weights(z_ref, w_ref, (hh,))
            scores(jnp.maximum(kb - 1, 0), z_next_ref, (hh,))

    def two_steps(p, c):
        kb = qi - 1 - 2 * p
        step(kb, z1_ref, w1_ref, z0_ref, w0_ref)
        step(kb - 1, z0_ref, w0_ref, z1_ref, w1_ref)
        return c

    lax.fori_loop(0, qi // 2, two_steps, 0)

    @pl.when(qi % 2 == 1)
    def _():
        for hh in all_heads:
            weighted_values(1, w0_ref, (hh,))
            weights(z1_ref, w1_ref, (hh,))
        weighted_values(0, w1_ref)

    @pl.when(qi % 2 == 0)
    def _():
        weighted_values(0, w0_ref)

    o_ref[...] = acc_ref[...].astype(o_ref.dtype)


def _sb_attention(qkv, *, batch, heads, head_dim, t=256, heads_per_block=8):
    n = qkv.shape[0]
    seq = n // batch
    t = min(t, seq)
    hpb = min(heads_per_block, heads)
    assert seq % t == 0 and t % LANES == 0 and head_dim == LANES and heads % hpb == 0
    nq = seq // t
    nhb = heads // hpb
    wb = hpb * head_dim
    tri = (jnp.arange(LANES)[:, None] > jnp.arange(LANES)[None, :]).astype(bf16)
    m = jnp.concatenate([tri, jnp.ones((LANES, LANES), bf16)], axis=1)
    m = jnp.concatenate([m, m], axis=0)
    return pl.pallas_call(
        functools.partial(_sb_kernel, heads_per_block=hpb),
        out_shape=jax.ShapeDtypeStruct((n, heads * head_dim), bf16),
        grid=(batch, nhb, nq),
        in_specs=[
            pl.BlockSpec((t, wb), lambda b, h, i: (b * nq + i, h)),
            pl.BlockSpec((seq, wb), lambda b, h, i: (b, nhb + h)),
            pl.BlockSpec((seq, wb), lambda b, h, i: (b, 2 * nhb + h)),
            pl.BlockSpec((2 * LANES, 2 * LANES), lambda b, h, i: (0, 0)),
        ],
        out_specs=pl.BlockSpec((t, wb), lambda b, h, i: (b * nq + i, h)),
        scratch_shapes=[pltpu.VMEM((t, wb), f32), pltpu.VMEM((t, wb), f32),
                        pltpu.VMEM((t, hpb * t), f32), pltpu.VMEM((t, hpb * t), f32),
                        pltpu.VMEM((t, hpb * t), bf16), pltpu.VMEM((t, hpb * t), bf16)],
        compiler_params=_params("parallel", "parallel", "arbitrary"),
        name="sb_attention",
    )(qkv, qkv, qkv, m)


def kernel(x, norm_ffn1, ffn1_w_gate, ffn1_w_up, ffn1_w_down, norm_mix, s5_w_in, s5_lam_re, s5_lam_im, s5_log_dt, s5_b_re, s5_b_im, s5_c_re, s5_c_im, s5_d, s5_w_glu, s5_b_glu, s5_w_out, sb_w_qkv, sb_g_q, sb_g_k, sb_w_o, norm_ffn2, ffn2_w_gate, ffn2_w_up, ffn2_w_down):
    batch, seq, d = x.shape
    depth = norm_ffn1.shape[0]
    head_dim = sb_g_q.shape[1]
    heads = sb_w_o.shape[1] // head_dim
    n_mixers = 2
    xf = x.reshape(batch * seq, d).astype(f32)
    for i in range(depth):
        xf = _ffn(xf, norm_ffn1[i], _layer_bf16(ffn1_w_gate, i), _layer_bf16(ffn1_w_up, i),
                  _layer_bf16(ffn1_w_down, i))
        j = i // n_mixers
        if i % n_mixers == 0:
            u = _norm_matmul(xf, norm_mix[i], _layer_bf16(s5_w_in, j), out_dtype=f32)
            y, yb = _s5_core(u, s5_lam_re[j], s5_lam_im[j], s5_log_dt[j], s5_b_re[j], s5_b_im[j],
                             s5_c_re[j], s5_c_im[j], s5_d[j], batch=batch)
            xf = _glu_out(yb, y, xf, _layer_bf16(s5_w_glu, j), s5_b_glu[j].astype(f32), _layer_bf16(s5_w_out, j))
        else:
            hd = heads * head_dim
            q_scale = LOG2E / math.sqrt(head_dim)
            head_gain = jnp.concatenate([jnp.tile(sb_g_q[j].astype(f32) * q_scale, heads),
                                         jnp.tile(sb_g_k[j].astype(f32), heads)]).reshape(1, 2 * hd)
            qkv = _qkv(xf, norm_mix[i], _layer_bf16(sb_w_qkv, j), head_gain, 2 * hd, head_dim)
            o = _sb_attention(qkv, batch=batch, heads=heads, head_dim=head_dim)
            xf = _matmul_res(o, _layer_bf16(sb_w_o, j), xf)
        xf = _ffn(xf, norm_ffn2[i], _layer_bf16(ffn2_w_gate, i), _layer_bf16(ffn2_w_up, i),
                  _layer_bf16(ffn2_w_down, i))
    return xf.reshape(batch, seq, d).astype(x.dtype)
```

```python
import functools
import math

import jax
import jax.numpy as jnp
from jax import lax
from jax.experimental import pallas as pl
from jax.experimental.pallas import tpu as pltpu

EPS = 1e-6
FFN_RES = 0.5
S5_LAM_RE_MAX = -1e-4
LOG2E = 1.4426950408889634
LANES = 128
SUBLANES = 8
VMEM_LIMIT = 56 * 1024 * 1024
S5_GROUPS_PER_BLOCK = 16

f32 = jnp.float32
bf16 = jnp.bfloat16


def _params(*sem):
    return pltpu.CompilerParams(dimension_semantics=sem, vmem_limit_bytes=VMEM_LIMIT)


def _rmsnorm(x, g):
    ms = jnp.mean(x * x, axis=-1, keepdims=True)
    return x * lax.rsqrt(ms + EPS) * g


CAST_BLOCK_BYTES = 12 * 1024 * 1024


def _cast_kernel(w_ref, o_ref):
    o_ref[...] = w_ref[...].astype(o_ref.dtype)


def _layer_bf16(w, layer):
    _, r, c = w.shape
    tr = r
    while tr % 2 == 0 and tr > SUBLANES and tr * c * w.dtype.itemsize > CAST_BLOCK_BYTES:
        tr //= 2
    return pl.pallas_call(
        _cast_kernel,
        out_shape=jax.ShapeDtypeStruct((r, c), bf16),
        grid=(r // tr,),
        in_specs=[pl.BlockSpec((None, tr, c), lambda i: (layer, i, 0))],
        out_specs=pl.BlockSpec((tr, c), lambda i: (i, 0)),
        compiler_params=_params("parallel"),
        name="layer_bf16",
    )(w)


def _ffn_kernel(x_ref, g_ref, wg_ref, wu_ref, wd_ref, o_ref, h_ref):
    @pl.when(pl.program_id(1) == 0)
    def _():
        x = x_ref[...]
        h_ref[...] = _rmsnorm(x, g_ref[...]).astype(bf16)
        o_ref[...] = x

    h = h_ref[...]
    gate = jnp.dot(h, wg_ref[...], preferred_element_type=f32)
    up = jnp.dot(h, wu_ref[...], preferred_element_type=f32)
    a = (gate * jax.nn.sigmoid(gate) * up * FFN_RES).astype(bf16)
    o_ref[...] += jnp.dot(a, wd_ref[...], preferred_element_type=f32)


def _ffn(x, g, wg, wu, wd, *, tm=1024, tf=512):
    n, d = x.shape
    f = wg.shape[1]
    tm, tf = min(tm, n), min(tf, f)
    assert n % tm == 0 and f % tf == 0
    return pl.pallas_call(
        _ffn_kernel,
        out_shape=jax.ShapeDtypeStruct((n, d), f32),
        grid=(n // tm, f // tf),
        in_specs=[
            pl.BlockSpec((tm, d), lambda i, j: (i, 0)),
            pl.BlockSpec((1, d), lambda i, j: (0, 0)),
            pl.BlockSpec((d, tf), lambda i, j: (0, j)),
            pl.BlockSpec((d, tf), lambda i, j: (0, j)),
            pl.BlockSpec((tf, d), lambda i, j: (j, 0)),
        ],
        out_specs=pl.BlockSpec((tm, d), lambda i, j: (i, 0)),
        scratch_shapes=[pltpu.VMEM((tm, d), bf16)],
        compiler_params=_params("parallel", "arbitrary"),
        name="ffn",
    )(x, g.reshape(1, d), wg, wu, wd)


def _norm_matmul_kernel(x_ref, g_ref, w_ref, o_ref, h_ref):
    @pl.when(pl.program_id(1) == 0)
    def _():
        h_ref[...] = _rmsnorm(x_ref[...], g_ref[...]).astype(bf16)

    o_ref[...] = jnp.dot(h_ref[...], w_ref[...], preferred_element_type=f32).astype(o_ref.dtype)


def _norm_matmul(x, g, w, *, out_dtype, tm=1024, tn=2048):
    n, d = x.shape
    dout = w.shape[1]
    tm, tn = min(tm, n), min(tn, dout)
    assert n % tm == 0 and dout % tn == 0
    w_mode = pl.Buffered(1) if tn == dout else None
    return pl.pallas_call(
        _norm_matmul_kernel,
        out_shape=jax.ShapeDtypeStruct((n, dout), out_dtype),
        grid=(n // tm, dout // tn),
        in_specs=[
            pl.BlockSpec((tm, d), lambda i, j: (i, 0)),
            pl.BlockSpec((1, d), lambda i, j: (0, 0)),
            pl.BlockSpec((d, tn), lambda i, j: (0, j), pipeline_mode=w_mode),
        ],
        out_specs=pl.BlockSpec((tm, tn), lambda i, j: (i, j)),
        scratch_shapes=[pltpu.VMEM((tm, d), bf16)],
        compiler_params=_params("parallel", "arbitrary"),
        name="norm_matmul",
    )(x, g.reshape(1, d), w)


def _qkv_kernel(x_ref, g_ref, w_ref, hg_ref, o_ref, h_ref, *, n_norm_tiles, head_dim):
    j = pl.program_id(1)

    @pl.when(j == 0)
    def _():
        h_ref[...] = _rmsnorm(x_ref[...], g_ref[...]).astype(bf16)

    acc = jnp.dot(h_ref[...], w_ref[...], preferred_element_type=f32)
    normed = j < n_norm_tiles
    hg = hg_ref[...]
    for c in range(0, acc.shape[1], head_dim):
        a = acc[:, c:c + head_dim]
        o_ref[:, c:c + head_dim] = jnp.where(normed, _rmsnorm(a, hg[:, c:c + head_dim]), a).astype(o_ref.dtype)


def _qkv(x, g, w, head_gain, n_norm_cols, head_dim, *, tm=1024, tn=2048):
    n, d = x.shape
    dout = w.shape[1]
    tm, tn = min(tm, n), math.gcd(tn, math.gcd(dout, n_norm_cols))
    assert n % tm == 0 and dout % tn == 0 and n_norm_cols % tn == 0 and tn % head_dim == 0
    n_norm_tiles = n_norm_cols // tn
    return pl.pallas_call(
        functools.partial(_qkv_kernel, n_norm_tiles=n_norm_tiles, head_dim=head_dim),
        out_shape=jax.ShapeDtypeStruct((n, dout), bf16),
        grid=(n // tm, dout // tn),
        in_specs=[
            pl.BlockSpec((tm, d), lambda i, j: (i, 0)),
            pl.BlockSpec((1, d), lambda i, j: (0, 0)),
            pl.BlockSpec((d, tn), lambda i, j: (0, j)),
            pl.BlockSpec((1, tn), lambda i, j: (0, jnp.minimum(j, n_norm_tiles - 1))),
        ],
        out_specs=pl.BlockSpec((tm, tn), lambda i, j: (i, j)),
        scratch_shapes=[pltpu.VMEM((tm, d), bf16)],
        compiler_params=_params("parallel", "arbitrary"),
        name="qkv",
    )(x, g.reshape(1, d), w, head_gain)


def _matmul_res_kernel(a_ref, w_ref, x_ref, o_ref):
    o_ref[...] = x_ref[...] + jnp.dot(a_ref[...], w_ref[...], preferred_element_type=f32)


def _matmul_res(a, w, x, *, tm=1024, tn=2048):
    n, k = a.shape
    d = w.shape[1]
    tm, tn = min(tm, n), min(tn, d)
    assert n % tm == 0 and d % tn == 0
    w_mode = pl.Buffered(1) if tn == d else None
    return pl.pallas_call(
        _matmul_res_kernel,
        out_shape=jax.ShapeDtypeStruct((n, d), f32),
        grid=(n // tm, d // tn),
        in_specs=[
            pl.BlockSpec((tm, k), lambda i, j: (i, 0)),
            pl.BlockSpec((k, tn), lambda i, j: (0, j), pipeline_mode=w_mode),
            pl.BlockSpec((tm, tn), lambda i, j: (i, j)),
        ],
        out_specs=pl.BlockSpec((tm, tn), lambda i, j: (i, j)),
        compiler_params=_params("parallel", "arbitrary"),
        name="matmul_res",
    )(a, w, x)


def _glu_out_kernel(yb_ref, yj_ref, x_ref, wg_ref, b_ref, wo_ref, o_ref, acc_ref):
    slabs, ti, c = yb_ref.shape
    j = pl.program_id(1)

    @pl.when(j == 0)
    def _():
        acc_ref[...] = jnp.zeros_like(acc_ref)

    z = jnp.dot(yb_ref[...].reshape(slabs * ti, c), wg_ref[...], preferred_element_type=f32) + b_ref[...]
    a = (yj_ref[...].reshape(slabs * ti, -1) * jax.nn.sigmoid(z)).astype(bf16)
    acc_ref[...] += jnp.dot(a, wo_ref[...], preferred_element_type=f32)

    @pl.when(j == pl.num_programs(1) - 1)
    def _():
        for s in range(slabs):
            o_ref[:, s, :] = x_ref[:, s, :] + acc_ref[pl.ds(s * ti, ti), :]


def _glu_out(yb, y, x, w_glu, b_glu, w_out, *, ti=128, tn=512):
    slabs, ns, c = y.shape
    n = slabs * ns
    d = w_out.shape[1]
    ti, tn = min(ti, ns), min(tn, c)
    assert ns % ti == 0 and c % tn == 0
    out = pl.pallas_call(
        _glu_out_kernel,
        out_shape=jax.ShapeDtypeStruct((ns, slabs, d), f32),
        grid=(ns // ti, c // tn),
        in_specs=[
            pl.BlockSpec((slabs, ti, c), lambda i, j: (0, i, 0)),
            pl.BlockSpec((slabs, ti, tn), lambda i, j: (0, i, j)),
            pl.BlockSpec((ti, slabs, d), lambda i, j: (i, 0, 0), pipeline_mode=pl.Buffered(1)),
            pl.BlockSpec((c, tn), lambda i, j: (0, j)),
            pl.BlockSpec((1, tn), lambda i, j: (0, j)),
            pl.BlockSpec((tn, d), lambda i, j: (j, 0)),
        ],
        out_specs=pl.BlockSpec((ti, slabs, d), lambda i, j: (i, 0, 0)),
        scratch_shapes=[pltpu.VMEM((slabs * ti, d), f32)],
        compiler_params=_params("parallel", "arbitrary"),
        name="glu_out",
    )(yb, y, x.reshape(ns, slabs, d), w_glu, b_glu.reshape(1, c), w_out)
    return out.reshape(n, d)


S5_BLOCK = 8


def _s5_kernel(*refs, half, n_u):
    u_refs = refs[:n_u]
    wb_ref, kt_ref, wc_ref, lam_ref, d_ref, o_ref, ob_ref, uf_ref, ub_ref, s_ref, cb_ref, carry_ref = refs[n_u:]
    _, tr, cb = uf_ref.shape

    @pl.when(pl.program_id(2) == 0)
    def _():
        carry_ref[...] = jnp.zeros_like(carry_ref)

    for r in range(S5_BLOCK):
        for k in range(n_u):
            ur = u_refs[k][pl.ds(r, tr, stride=S5_BLOCK), :]
            uf_ref[r, :, pl.ds(k * LANES, LANES)] = ur
            ub_ref[:, pl.ds((S5_BLOCK - 1 - r) * cb + k * LANES, LANES)] = ur.astype(bf16)
    s_ref[...] = jnp.dot(ub_ref[...], wb_ref[0], preferred_element_type=f32)

    n_lane_tiles = half // LANES
    last = SUBLANES - 1
    first_row = lax.broadcasted_iota(jnp.int32, (SUBLANES, LANES), 0) == 0

    for j in range(S5_BLOCK):
        o_ref[j] = (jnp.dot(ub_ref[:, (S5_BLOCK - 1 - j) * cb:], kt_ref[0, :(j + 1) * cb, :],
                            preferred_element_type=f32) + d_ref[...] * uf_ref[j])

    def row_tile(i, carry):
        rows = pl.ds(i * SUBLANES, SUBLANES)
        out = []
        for c in range(n_lane_tiles):
            re_l = pl.ds(c * LANES, LANES)
            im_l = pl.ds(half + c * LANES, LANES)
            cre, cim = carry[2 * c], carry[2 * c + 1]
            xre = s_ref[rows, re_l]
            xim = s_ref[rows, im_l]
            for k, dist in enumerate((1, 2, 4)):
                lre = lam_ref[0, 2 * k, :, re_l]
                lim = lam_ref[0, 2 * k + 1, :, re_l]
                pre = pltpu.roll(xre, dist, 0)
                pim = pltpu.roll(xim, dist, 0)
                xre, xim = xre + (lre * pre - lim * pim), xim + (lre * pim + lim * pre)
            are = lam_ref[0, 6, :, re_l]
            aim = lam_ref[0, 7, :, re_l]
            xre = xre + (are * cre - aim * cim)
            xim = xim + (are * cim + aim * cre)
            s_ref[rows, re_l] = jnp.where(first_row, cre, pltpu.roll(xre, 1, 0))
            s_ref[rows, im_l] = jnp.where(first_row, cim, pltpu.roll(xim, 1, 0))
            out.append(jnp.broadcast_to(xre[last:last + 1, :], xre.shape))
            out.append(jnp.broadcast_to(xim[last:last + 1, :], xim.shape))
        return tuple(out)

    carry0 = []
    for c in range(n_lane_tiles):
        carry0.append(carry_ref[:, pl.ds(c * LANES, LANES)])
        carry0.append(carry_ref[:, pl.ds(half + c * LANES, LANES)])
    carry = tuple(carry0)
    for i in range(tr // SUBLANES):
        carry = row_tile(i, carry)
    for c in range(n_lane_tiles):
        carry_ref[:, pl.ds(c * LANES, LANES)] = carry[2 * c]
        carry_ref[:, pl.ds(half + c * LANES, LANES)] = carry[2 * c + 1]

    cb_ref[...] = s_ref[...].astype(bf16)
    for j in range(S5_BLOCK):
        y = jax.nn.gelu(o_ref[j] + jnp.dot(cb_ref[...], wc_ref[0, j], preferred_element_type=f32))
        o_ref[j] = y
        ob_ref[j] = y.astype(bf16)


def _s5_discretise(lam_re, lam_im, log_dt, b_re, b_im):
    dt = jnp.exp(log_dt.astype(f32))[:, None]
    lr = jnp.minimum(lam_re.astype(f32), S5_LAM_RE_MAX)
    li = lam_im.astype(f32)
    mag = jnp.exp(lr * dt)
    ab_re = mag * jnp.cos(li * dt)
    ab_im = mag * jnp.sin(li * dt)
    den = lr * lr + li * li
    n_re = ab_re - 1.0
    f_re = (n_re * lr + ab_im * li) / den
    f_im = (ab_im * lr - n_re * li) / den
    br = b_re.astype(f32)
    bi = b_im.astype(f32)
    bb_re = f_re[..., None] * br - f_im[..., None] * bi
    bb_im = f_re[..., None] * bi + f_im[..., None] * br
    return lr * dt, li * dt, bb_re, bb_im


EXPAND_PER_STEP = 8


def _expand_kernel(v_ref, e_ref, o_ref, *, row_group, col_group, groups, lhs_transposed):
    rows = lax.broadcasted_iota(jnp.int32, o_ref.shape[1:], 0)
    cols = lax.broadcasted_iota(jnp.int32, o_ref.shape[1:], 1)
    row_g = lax.shift_right_logical(rows, row_group.bit_length() - 1) & (groups - 1)
    col_g = lax.shift_right_logical(cols, col_group.bit_length() - 1) & (groups - 1)
    same_group = row_g == col_g
    for i in range(v_ref.shape[0]):
        if lhs_transposed:
            t = lax.dot_general(v_ref[i], e_ref[...], (((0,), (0,)), ((), ())), preferred_element_type=f32)
        else:
            t = jnp.dot(v_ref[i], e_ref[...], preferred_element_type=f32)
        o_ref[i] = jnp.where(same_group, t, 0.0).astype(o_ref.dtype)


def _expand_block_diag(v, e, *, row_group, col_group, groups, lhs_transposed=False):
    nlead = v.shape[0]
    rows = v.shape[2] if lhs_transposed else v.shape[1]
    cols = e.shape[1]
    for size in (row_group, col_group, groups):
        assert size & (size - 1) == 0
    per_step = math.gcd(nlead, EXPAND_PER_STEP)
    return pl.pallas_call(
        functools.partial(_expand_kernel, row_group=row_group, col_group=col_group, groups=groups,
                          lhs_transposed=lhs_transposed),
        out_shape=jax.ShapeDtypeStruct((nlead, rows, cols), bf16),
        grid=(nlead // per_step,),
        in_specs=[pl.BlockSpec((per_step,) + v.shape[1:], lambda i: (i, 0, 0)),
                  pl.BlockSpec(e.shape, lambda i: (0, 0))],
        out_specs=pl.BlockSpec((per_step, rows, cols), lambda i: (i, 0, 0)),
        compiler_params=_params("parallel"),
        name="expand_block_diag",
    )(v, e)


def _s5_scan_consts(z_re, z_im, gpb):
    g, p = z_re.shape

    def power(k):
        m = jnp.exp(z_re * k)
        return m * jnp.cos(z_im * k), m * jnp.sin(z_im * k)

    sub = jnp.arange(SUBLANES)[:, None, None]
    rows = []
    for dist in (1, 2, 4):
        pr, pi = power(float(dist))
        rows.append(jnp.where(sub >= dist, pr[None], 0.0))
        rows.append(jnp.where(sub >= dist, pi[None], 0.0))
    ks = (jnp.arange(SUBLANES, dtype=f32) + 1.0)[:, None, None]
    m = jnp.exp(z_re[None] * ks)
    rows.append(m * jnp.cos(z_im[None] * ks))
    rows.append(m * jnp.sin(z_im[None] * ks))
    c = jnp.stack(rows)
    c = c.reshape(8, SUBLANES, g // gpb, gpb * p)
    return c.transpose(2, 0, 1, 3)


def _s5_core(u, lam_re, lam_im, log_dt, b_re, b_im, c_re, c_im, d_skip, *, batch, tr=256):
    n, ch = u.shape
    g, p, h = b_re.shape
    nblk = n // S5_BLOCK
    blk_per_seq = nblk // batch
    gpb = min(S5_GROUPS_PER_BLOCK, g)
    tr = min(tr, blk_per_seq)
    assert n % (S5_BLOCK * batch) == 0 and g % gpb == 0 and blk_per_seq % tr == 0
    assert tr % SUBLANES == 0 and (gpb * p) % LANES == 0 and (gpb * h) % LANES == 0
    nb = g // gpb
    half = gpb * p
    cb = gpb * h
    nt = blk_per_seq // tr
    z_re, z_im, bb_re, bb_im = _s5_discretise(lam_re, lam_im, log_dt, b_re, b_im)
    cr = c_re.astype(f32)
    ci = c_im.astype(f32)

    ks = jnp.arange(S5_BLOCK + 1, dtype=f32)[:, None, None]
    mag = jnp.exp(z_re[None] * ks)
    pw_re = mag * jnp.cos(z_im[None] * ks)
    pw_im = mag * jnp.sin(z_im[None] * ks)
    lb_re = pw_re[:-1, :, :, None] * bb_re[None] - pw_im[:-1, :, :, None] * bb_im[None]
    lb_im = pw_re[:-1, :, :, None] * bb_im[None] + pw_im[:-1, :, :, None] * bb_re[None]
    lane = jnp.arange(2 * half)
    col = jnp.arange(2 * p)
    rep_state = ((lane[None, :] // half == col[:, None] // p) & (lane[None, :] % p == col[:, None] % p)).astype(bf16)
    rep_chan = (jnp.arange(cb)[None, :] % h == jnp.arange(h)[:, None]).astype(bf16)
    wb = jnp.stack([lb_re, lb_im], axis=2)
    wb = wb.reshape(S5_BLOCK, nb, gpb, 2, p, h).transpose(1, 0, 2, 5, 3, 4)
    wb = _expand_block_diag(wb.reshape(nb * S5_BLOCK, cb, 2 * p).astype(bf16), rep_state,
                            row_group=h, col_group=p, groups=gpb).reshape(nb, S5_BLOCK * cb, 2 * half)
    kt = jnp.einsum("gop,tgpi->tgio", cr, lb_re) - jnp.einsum("gop,tgpi->tgio", ci, lb_im)
    kt = kt.reshape(S5_BLOCK, nb, gpb, h, h).transpose(1, 0, 2, 3, 4)
    kt = _expand_block_diag(kt.reshape(nb * S5_BLOCK, cb, h).astype(bf16), rep_chan,
                            row_group=h, col_group=h, groups=gpb).reshape(nb, S5_BLOCK * cb, cb)
    cl_re = cr[None] * pw_re[1:, :, None, :] - ci[None] * pw_im[1:, :, None, :]
    cl_im = cr[None] * pw_im[1:, :, None, :] + ci[None] * pw_re[1:, :, None, :]
    wc = jnp.stack([cl_re, -cl_im], axis=2)
    wc = wc.reshape(S5_BLOCK, nb, gpb, 2, h, p).transpose(1, 0, 4, 3, 2, 5)
    wc = _expand_block_diag(wc.reshape(nb * S5_BLOCK, h, 2 * half).astype(bf16), rep_chan, row_group=p,
                            col_group=h, groups=gpb, lhs_transposed=True).reshape(nb, S5_BLOCK, 2 * half, cb)
    lam = _s5_scan_consts(z_re * S5_BLOCK, z_im * S5_BLOCK, gpb)

    once = dict(pipeline_mode=pl.Buffered(1))
    n_u = cb // LANES
    u_specs = [pl.BlockSpec((S5_BLOCK * tr, LANES), lambda j, b, t, k=k: (b * nt + t, j * n_u + k))
               for k in range(n_u)]
    out_spec = pl.BlockSpec((S5_BLOCK, tr, cb), lambda j, b, t: (0, b * nt + t, j))
    return pl.pallas_call(
        functools.partial(_s5_kernel, half=half, n_u=n_u),
        out_shape=(jax.ShapeDtypeStruct((S5_BLOCK, nblk, ch), f32),
                   jax.ShapeDtypeStruct((S5_BLOCK, nblk, ch), bf16)),
        grid=(nb, batch, nt),
        in_specs=u_specs + [
            pl.BlockSpec((1, S5_BLOCK * cb, 2 * half), lambda j, b, t: (j, 0, 0), **once),
            pl.BlockSpec((1, S5_BLOCK * cb, cb), lambda j, b, t: (j, 0, 0), **once),
            pl.BlockSpec((1, S5_BLOCK, 2 * half, cb), lambda j, b, t: (j, 0, 0, 0), **once),
            pl.BlockSpec((1, 8, SUBLANES, half), lambda j, b, t: (j, 0, 0, 0)),
            pl.BlockSpec((1, cb), lambda j, b, t: (0, j)),
        ],
        out_specs=(out_spec, out_spec),
        scratch_shapes=[pltpu.VMEM((S5_BLOCK, tr, cb), f32),
                        pltpu.VMEM((tr, S5_BLOCK * cb), bf16), pltpu.VMEM((tr, 2 * half), f32),
                        pltpu.VMEM((tr, 2 * half), bf16), pltpu.VMEM((SUBLANES, 2 * half), f32)],
        compiler_params=_params("parallel", "parallel", "arbitrary"),
        name="s5_core",
    )(*([u] * n_u), wb, kt, wc, lam, d_skip.reshape(1, ch).astype(f32))


def _sb_kernel(q_ref, k_ref, v_ref, m_ref, o_ref, acc_ref, carry_ref, z0_ref, z1_ref, w0_ref, w1_ref,
               *, heads_per_block):
    t = q_ref.shape[0]
    nsub = t // LANES
    qi = pl.program_id(2)
    m = m_ref[...]
    row = lax.broadcasted_iota(jnp.int32, (t, t), 0)
    col = lax.broadcasted_iota(jnp.int32, (t, t), 1)
    past = col < row
    acc_ref[...] = jnp.zeros_like(acc_ref)
    carry_ref[...] = jnp.zeros_like(carry_ref)

    def key_rows(kb):
        return pl.ds(pl.multiple_of(kb * t, t), t)

    all_heads = range(heads_per_block)

    def scores(kb, z_ref, heads=all_heads):
        ks = key_rows(kb)
        for hh in heads:
            z_ref[:, pl.ds(hh * t, t)] = lax.dot_general(
                q_ref[:, pl.ds(hh * LANES, LANES)], k_ref[ks, pl.ds(hh * LANES, LANES)],
                (((1,), (1,)), ((), ())), preferred_element_type=f32)

    def block_weights(z, run, mask):
        zneg = jnp.minimum(z, 0.0)
        zdiff = zneg - z
        soft = jnp.log(1.0 + jnp.exp2(zdiff + zneg)) * LOG2E
        log_beta = zneg - soft
        log_keep = zdiff - soft
        if mask is not None:
            log_keep = jnp.where(mask, log_keep, 0.0)
        hi = log_keep.astype(bf16)
        lo = (log_keep - hi.astype(f32)).astype(bf16)
        cs = jnp.dot(jnp.concatenate([hi, lo], axis=1), m, preferred_element_type=f32)
        w = jnp.exp2(log_beta + cs[:, :LANES] + run)
        if mask is not None:
            w = jnp.where(mask, w, 0.0)
        return w.astype(bf16), cs[:, LANES:]

    def weights(z_ref, w_ref, heads=all_heads):
        for hh in heads:
            hl = pl.ds(hh * LANES, LANES)
            run = carry_ref[:, hl]
            for j in reversed(range(nsub)):
                cols = pl.ds(hh * t + j * LANES, LANES)
                w, row_sum = block_weights(z_ref[:, cols], run, None)
                w_ref[:, cols] = w
                run = run + row_sum
            carry_ref[:, hl] = run

    def diagonal_weights(z_ref, w_ref, heads=all_heads):
        diag = past[:LANES, :LANES]
        for hh in heads:
            hl = pl.ds(hh * LANES, LANES)
            for i in range(nsub):
                rows = pl.ds(i * LANES, LANES)
                run = carry_ref[rows, hl]
                for j in reversed(range(nsub)):
                    cols = pl.ds(hh * t + j * LANES, LANES)
                    if j > i:
                        w_ref[rows, cols] = jnp.zeros((LANES, LANES), bf16)
                        continue
                    w, row_sum = block_weights(z_ref[rows, cols], run, diag if j == i else None)
                    w_ref[rows, cols] = w
                    run = run + row_sum
                carry_ref[rows, hl] = run

    def weighted_values(kb, w_ref, heads=all_heads):
        ks = key_rows(kb)
        for hh in heads:
            hl = pl.ds(hh * LANES, LANES)
            acc_ref[:, hl] += jnp.dot(w_ref[:, pl.ds(hh * t, t)], v_ref[ks, hl], preferred_element_type=f32)

    for hh in all_heads:
        scores(qi, z0_ref, (hh,))
        scores(jnp.maximum(qi - 1, 0), z1_ref, (hh,))
    for hh in all_heads:
        diagonal_weights(z0_ref, w0_ref, (hh,))

    def step(kb, z_ref, w_ref, z_next_ref, w_prev_ref):
        for hh in all_heads:
            weighted_values(kb + 1, w_prev_ref, (hh,))
            weights(z_ref, w_ref, (hh,))
            scores(jnp.maximum(kb - 1, 0), z_next_ref, (hh,))

    def two_steps(p, c):
        kb = qi - 1 - 2 * p
        step(kb, z1_ref, w1_ref, z0_ref, w0_ref)
        step(kb - 1, z0_ref, w0_ref, z1_ref, w1_ref)
        return c

    lax.fori_loop(0, qi // 2, two_steps, 0)

    @pl.when(qi % 2 == 1)
    def _():
        for hh in all_heads:
            weighted_values(1, w0_ref, (hh,))
            weights(z1_ref, w1_ref, (hh,))
        weighted_values(0, w1_ref)

    @pl.when(qi % 2 == 0)
    def _():
        weighted_values(0, w0_ref)

    o_ref[...] = acc_ref[...].astype(o_ref.dtype)


def _sb_attention(qkv, *, batch, heads, head_dim, t=256, heads_per_block=8):
    n = qkv.shape[0]
    seq = n // batch
    t = min(t, seq)
    hpb = min(heads_per_block, heads)
    assert seq % t == 0 and t % LANES == 0 and head_dim == LANES and heads % hpb == 0
    nq = seq // t
    nhb = heads // hpb
    wb = hpb * head_dim
    tri = (jnp.arange(LANES)[:, None] > jnp.arange(LANES)[None, :]).astype(bf16)
    m = jnp.concatenate([tri, jnp.ones((LANES, LANES), bf16)], axis=1)
    m = jnp.concatenate([m, m], axis=0)
    return pl.pallas_call(
        functools.partial(_sb_kernel, heads_per_block=hpb),
        out_shape=jax.ShapeDtypeStruct((n, heads * head_dim), bf16),
        grid=(batch, nhb, nq),
        in_specs=[
            pl.BlockSpec((t, wb), lambda b, h, i: (b * nq + i, h)),
            pl.BlockSpec((seq, wb), lambda b, h, i: (b, nhb + h)),
            pl.BlockSpec((seq, wb), lambda b, h, i: (b, 2 * nhb + h)),
            pl.BlockSpec((2 * LANES, 2 * LANES), lambda b, h, i: (0, 0)),
        ],
        out_specs=pl.BlockSpec((t, wb), lambda b, h, i: (b * nq + i, h)),
        scratch_shapes=[pltpu.VMEM((t, wb), f32), pltpu.VMEM((t, wb), f32),
                        pltpu.VMEM((t, hpb * t), f32), pltpu.VMEM((t, hpb * t), f32),
                        pltpu.VMEM((t, hpb * t), bf16), pltpu.VMEM((t, hpb * t), bf16)],
        compiler_params=_params("parallel", "parallel", "arbitrary"),
        name="sb_attention",
    )(qkv, qkv, qkv, m)


def kernel(x, norm_ffn1, ffn1_w_gate, ffn1_w_up, ffn1_w_down, norm_mix, s5_w_in, s5_lam_re, s5_lam_im, s5_log_dt, s5_b_re, s5_b_im, s5_c_re, s5_c_im, s5_d, s5_w_glu, s5_b_glu, s5_w_out, sb_w_qkv, sb_g_q, sb_g_k, sb_w_o, norm_ffn2, ffn2_w_gate, ffn2_w_up, ffn2_w_down):
    batch, seq, d = x.shape
    depth = norm_ffn1.shape[0]
    head_dim = sb_g_q.shape[1]
    heads = sb_w_o.shape[1] // head_dim
    n_mixers = 2
    xf = x.reshape(batch * seq, d).astype(f32)
    for i in range(depth):
        xf = _ffn(xf, norm_ffn1[i], _layer_bf16(ffn1_w_gate, i), _layer_bf16(ffn1_w_up, i),
                  _layer_bf16(ffn1_w_down, i))
        j = i // n_mixers
        if i % n_mixers == 0:
            u = _norm_matmul(xf, norm_mix[i], _layer_bf16(s5_w_in, j), out_dtype=f32)
            y, yb = _s5_core(u, s5_lam_re[j], s5_lam_im[j], s5_log_dt[j], s5_b_re[j], s5_b_im[j],
                             s5_c_re[j], s5_c_im[j], s5_d[j], batch=batch)
            xf = _glu_out(yb, y, xf, _layer_bf16(s5_w_glu, j), s5_b_glu[j].astype(f32), _layer_bf16(s5_w_out, j))
        else:
            hd = heads * head_dim
            q_scale = LOG2E / math.sqrt(head_dim)
            head_gain = jnp.concatenate([jnp.tile(sb_g_q[j].astype(f32) * q_scale, heads),
                                         jnp.tile(sb_g_k[j].astype(f32), heads)]).reshape(1, 2 * hd)
            qkv = _qkv(xf, norm_mix[i], _layer_bf16(sb_w_qkv, j), head_gain, 2 * hd, head_dim)
            o = _sb_attention(qkv, batch=batch, heads=heads, head_dim=head_dim)
            xf = _matmul_res(o, _layer_bf16(sb_w_o, j), xf)
        xf = _ffn(xf, norm_ffn2[i], _layer_bf16(ffn2_w_gate, i), _layer_bf16(ffn2_w_up, i),
                  _layer_bf16(ffn2_w_down, i))
    return xf.reshape(batch, seq, d).astype(x.dtype)
```
